```python
import math
import jax, jax.numpy as jnp
from jax import lax
import numpy as np

D_MODEL = 2048
BATCH = 1
SEQ = 8192
DEPTH = 1
DEC_BATCH = 32
DEC_SEQ = 4
PAST_LEN = 8192
PAGE_SIZE = 128

RWKV_WIDTH = D_MODEL // 2
RWKV_HEAD_SIZE = 64
RWKV_HEADS = RWKV_WIDTH // RWKV_HEAD_SIZE
DECAY_LORA = 64
AAA_LORA = 64
GATE_LORA = 160
RWKV_GN_EPS = 64e-5
NSA_WIDTH = D_MODEL // 2
HEAD_DIM = 64
NSA_HEADS = NSA_WIDTH // HEAD_DIM
NSA_KV_HEADS = 4
NSA_GROUP = NSA_HEADS // NSA_KV_HEADS
CMP_STRIDE = 16
CMP_BLOCK = 2 * CMP_STRIDE
CMP_RANK = 4
SEL_BLOCK = 64
N_SEL = 16
WINDOW = 512
Q_BLOCK = 128
ATTN_SCALE = HEAD_DIM ** -0.5
FFN_HIDDEN = 256 * math.ceil(8 * D_MODEL / 3 / 256)
DEEPNORM_ALPHA = (2 * DEPTH) ** 0.25
DEEPNORM_BETA = (8 * DEPTH) ** -0.25
LN_EPS = 1e-5
NEG_INF = -1e30
FORCE = 1e9
RWKV_COLS = 3 * RWKV_WIDTH + DECAY_LORA + AAA_LORA + GATE_LORA
Q_COLS = NSA_HEADS * HEAD_DIM
KV_COLS = 3 * 2 * NSA_KV_HEADS * HEAD_DIM
NSA_GATE_COLS = 3 * NSA_HEADS
MERGE_GATE_COLS = 2 * D_MODEL
IN_COLS = RWKV_COLS + Q_COLS + KV_COLS + NSA_GATE_COLS + MERGE_GATE_COLS

kernel_name = 'hybrid_rwkv7_nsa_macaron_deepnorm_step'


def layer_norm(x, g, b):
    xf = x.astype(jnp.float32)
    mu = xf.mean(-1, keepdims=True)
    var = jnp.square(xf - mu).mean(-1, keepdims=True)
    return ((xf - mu) * lax.rsqrt(var + LN_EPS) * g.astype(jnp.float32) + b.astype(jnp.float32)).astype(x.dtype)


def swiglu(x, w_gu, w_down):
    gate, up = jnp.split(x @ w_gu, 2, axis=-1)
    return (jax.nn.silu(gate) * up) @ w_down


def half_ffn_post_norm(x, w_gu, w_down, g, b):
    return layer_norm(DEEPNORM_ALPHA * x + 0.5 * swiglu(x, w_gu, w_down), g, b)


def gated_merge_post_norm(h, ya, yb, gm, w_o, g, b):
    mix = (gm[:, :, 0] * ya + gm[:, :, 1] * yb) @ w_o
    return layer_norm(DEEPNORM_ALPHA * h + mix, g, b)


def split_projection(u):
    bsz, t = u.shape[:2]
    o = 0
    ua = u[..., o:o + RWKV_COLS]; o += RWKV_COLS
    q = u[..., o:o + Q_COLS]; o += Q_COLS
    kv = u[..., o:o + KV_COLS]; o += KV_COLS
    gn = u[..., o:o + NSA_GATE_COLS]; o += NSA_GATE_COLS
    gm = u[..., o:o + MERGE_GATE_COLS]
    q = q.reshape(bsz, t, NSA_KV_HEADS, NSA_GROUP, HEAD_DIM)
    kv = kv.reshape(bsz, t, 3, 2, NSA_KV_HEADS, HEAD_DIM)
    gn = jax.nn.sigmoid(gn.reshape(bsz, t, 3, NSA_KV_HEADS, NSA_GROUP))
    gm = jax.nn.sigmoid(gm.reshape(bsz, t, 2, D_MODEL))
    return ua, q, kv, gn, gm


def rwkv7_time_mix(ua, prev_row, s0, mu, w0, w2, a0, a2, g2, k_k, k_a, r_k, ln_w, ln_b, w_up_a):
    bsz, t, _ = ua.shape
    f32 = jnp.float32
    uf = ua.astype(f32)
    prev = jnp.concatenate([prev_row.astype(f32)[:, None], uf[:, :-1]], axis=1)
    um = uf + (prev - uf) * mu.astype(f32)
    W = RWKV_WIDTH
    r, k, v = um[..., :W], um[..., W:2 * W], um[..., 2 * W:3 * W]
    o = 3 * W
    dw = um[..., o:o + DECAY_LORA]; o += DECAY_LORA
    da = um[..., o:o + AAA_LORA]; o += AAA_LORA
    dg = um[..., o:o + GATE_LORA]
    w_log = -jax.nn.softplus(-(w0.astype(f32) + jnp.tanh(dw) @ w2.astype(f32))) - 0.5
    decay = jnp.exp(-jnp.exp(w_log))
    a = jax.nn.sigmoid(a0.astype(f32) + da @ a2.astype(f32))
    g = jax.nn.sigmoid(dg) @ g2.astype(f32)
    H, N = RWKV_HEADS, RWKV_HEAD_SIZE
    heads = lambda z: z.reshape(bsz, t, H, N)
    r, k, v, decay, a = heads(r), heads(k), heads(v), heads(decay), heads(a)
    kk = k * k_k.astype(f32).reshape(H, N)
    kk = kk / jnp.maximum(jnp.sqrt(jnp.sum(kk * kk, -1, keepdims=True)), 1e-12)
    k = k * (1.0 + (a - 1.0) * k_a.astype(f32).reshape(H, N))

    def step(S, inp):
        r_t, w_t, k_t, v_t, kk_t, a_t = inp
        sa = jnp.einsum('bhvk,bhk->bhv', S, kk_t)
        S = (S * w_t[:, :, None, :] - sa[..., None] * (kk_t * a_t)[:, :, None, :]
             + v_t[..., None] * k_t[:, :, None, :])
        return S, jnp.einsum('bhvk,bhk->bhv', S, r_t)

    tm = lambda z: jnp.moveaxis(z, 1, 0)
    s_final, y = lax.scan(step, s0.astype(f32), (tm(r), tm(decay), tm(k), tm(v), tm(kk), tm(a)))
    y = jnp.moveaxis(y, 0, 1)
    ym = y.mean(-1, keepdims=True)
    yv = jnp.square(y - ym).mean(-1, keepdims=True)
    y = ((y - ym) * lax.rsqrt(yv + RWKV_GN_EPS)).reshape(bsz, t, W) * ln_w.astype(f32) + ln_b.astype(f32)
    bonus = (jnp.sum(r * k * r_k.astype(f32), -1, keepdims=True) * v).reshape(bsz, t, W)
    out = ((y + bonus) * g).astype(ua.dtype) @ w_up_a
    return out, s_final, ua[:, -1]


def compress_blocks(rows, cmp_pe, cmp_wa, cmp_wb):
    bsz, t = rows.shape[:2]
    n_ch = t // CMP_STRIDE
    ch = rows[:, :n_ch * CMP_STRIDE].reshape(bsz, n_ch, CMP_STRIDE, 2, NSA_KV_HEADS, HEAD_DIM)
    first = jnp.einsum('bnlshd,slr->bnrshd', ch, cmp_wa[:, :CMP_STRIDE])
    second = jnp.einsum('bnlshd,slr->bnrshd', ch, cmp_wa[:, CMP_STRIDE:])
    pe = jnp.einsum('sld,slr->rsd', cmp_pe, cmp_wa)[:, :, None, :]
    hid = jax.nn.gelu(first[:, :-1] + second[:, 1:] + pe)
    return jnp.einsum('bcrshd,srde->bcshe', hid, cmp_wb)


def nsa_compressed(q, pos, kc, vc):
    nc = kc.shape[1]
    s = jnp.einsum('bqhgd,bchd->bhgqc', q, kc).astype(jnp.float32) * ATTN_SCALE
    blk_end = jnp.arange(nc) * CMP_STRIDE + (CMP_BLOCK - 1)
    valid = blk_end[None, :] <= pos[:, None]
    p = jax.nn.softmax(jnp.where(valid, s, NEG_INF), axis=-1)
    p = jnp.where(valid, p, 0.0)
    return jnp.einsum('bhgqc,bchd->bqhgd', p.astype(vc.dtype), vc), p


def nsa_select(p, pos, n_blocks):
    bsz, kvh, _, tq, nc = p.shape
    ratio = SEL_BLOCK // CMP_STRIDE
    imp = jnp.pad(p.sum(axis=2), ((0, 0), (0, 0), (0, 0), (0, n_blocks * ratio - nc)))
    imp = imp.reshape(bsz, kvh, tq, n_blocks, ratio).sum(-1)
    blk = jnp.arange(n_blocks)
    cur = (pos // SEL_BLOCK)[:, None]
    forced = (blk == 0) | (blk == cur) | (blk == cur - 1)
    future = blk * SEL_BLOCK > pos[:, None]
    score = jnp.where(future, -FORCE, jnp.where(forced, FORCE, imp))
    _, idx = lax.top_k(score, min(N_SEL, n_blocks))
    return idx


def nsa_selected(q, pos, idx, ks, vs):
    s = jnp.einsum('bqhgd,bhqnld->bhgqnl', q, ks).astype(jnp.float32) * ATTN_SCALE
    kpos = idx[..., None] * SEL_BLOCK + jnp.arange(SEL_BLOCK)
    valid = (kpos <= pos[:, None, None])[:, :, None]
    bsz, kvh, g, tq, n, l = s.shape
    p = jax.nn.softmax(jnp.where(valid, s, NEG_INF).reshape(bsz, kvh, g, tq, n * l), axis=-1).reshape(s.shape)
    return jnp.einsum('bhgqnl,bhqnld->bqhgd', p.astype(vs.dtype), vs)


def nsa_window(q, pos, kw, vw, kpos):
    s = jnp.einsum('bqhgd,bkhd->bhgqk', q, kw).astype(jnp.float32) * ATTN_SCALE
    delta = pos[:, None] - kpos[None, :]
    valid = (delta >= 0) & (delta < WINDOW) & (kpos >= 0)[None, :]
    p = jax.nn.softmax(jnp.where(valid, s, NEG_INF), axis=-1)
    return jnp.einsum('bhgqk,bkhd->bqhgd', p.astype(vw.dtype), vw)


def nsa_combine(gn, o_c, o_s, o_w):
    return gn[:, :, 0, ..., None] * o_c + gn[:, :, 1, ..., None] * o_s + gn[:, :, 2, ..., None] * o_w


def nsa_prompt(q, kv, gn, cmp_pe, cmp_wa, cmp_wb):
    bsz, t = q.shape[:2]
    kvc = compress_blocks(kv[:, :, 0], cmp_pe, cmp_wa, cmp_wb)
    kc, vc = kvc[:, :, 0], kvc[:, :, 1]
    n_blocks = -(-t // SEL_BLOCK)
    sel = jnp.pad(kv[:, :, 1], ((0, 0), (0, n_blocks * SEL_BLOCK - t), (0, 0), (0, 0), (0, 0)))
    sel = sel.reshape(bsz, n_blocks, SEL_BLOCK, 2, NSA_KV_HEADS, HEAD_DIM)
    win = jnp.pad(kv[:, :, 2], ((0, 0), (WINDOW, 0), (0, 0), (0, 0), (0, 0)))
    bi = jnp.arange(bsz)[:, None, None, None]
    hi = jnp.arange(NSA_KV_HEADS)[None, :, None, None]
    n_qb = t // Q_BLOCK

    def query_block(args):
        qb, gb, start = args
        pos = start + jnp.arange(Q_BLOCK)
        o_c, p = nsa_compressed(qb, pos, kc, vc)
        idx = nsa_select(p, pos, n_blocks)
        blk = sel[bi, idx, :, :, hi, :]
        o_s = nsa_selected(qb, pos, idx, blk[..., 0, :], blk[..., 1, :])
        span = lax.dynamic_slice_in_dim(win, start, WINDOW + Q_BLOCK, axis=1)
        kpos = start - WINDOW + jnp.arange(WINDOW + Q_BLOCK)
        o_w = nsa_window(qb, pos, span[:, :, 0], span[:, :, 1], kpos)
        return nsa_combine(gb, o_c, o_s, o_w)

    to_blocks = lambda z: jnp.moveaxis(z.reshape(bsz, n_qb, Q_BLOCK, *z.shape[2:]), 1, 0)
    out = lax.map(query_block, (to_blocks(q), to_blocks(gn), jnp.arange(n_qb) * Q_BLOCK))
    return jnp.moveaxis(out, 0, 1).reshape(bsz, t, NSA_WIDTH)


def nsa_sample(q, kv, gn, cache_cmp_kv, cache_sel_kv, page_table, state_win_kv, cmp_pe, cmp_wa, cmp_wb):
    nb, nt = q.shape[:2]
    past = page_table.shape[1] * PAGE_SIZE
    pos = past + jnp.arange(nt)
    past_cmp = cache_cmp_kv[page_table].reshape(nb, past, 2, NSA_KV_HEADS, HEAD_DIM)
    rows_cmp = jnp.concatenate([past_cmp, kv[:, :, 0].astype(past_cmp.dtype)], axis=1)
    kvc = compress_blocks(rows_cmp, cmp_pe, cmp_wa, cmp_wb)
    o_c, p = nsa_compressed(q, pos, kvc[:, :, 0], kvc[:, :, 1])
    n_past_blk = past // SEL_BLOCK
    n_new_blk = -(-nt // SEL_BLOCK)
    idx = nsa_select(p, pos, n_past_blk + n_new_blk)
    bi = jnp.arange(nb)[:, None, None, None]
    hi = jnp.arange(NSA_KV_HEADS)[None, :, None, None]
    sub = PAGE_SIZE // SEL_BLOCK
    pool = cache_sel_kv.reshape(cache_sel_kv.shape[0], sub, SEL_BLOCK, 2, NSA_KV_HEADS, HEAD_DIM)
    ip = jnp.minimum(idx, n_past_blk - 1)
    blk_past = pool[page_table[bi, ip // sub], ip % sub, :, :, hi, :]
    new_rows = jnp.pad(kv[:, :, 1], ((0, 0), (0, n_new_blk * SEL_BLOCK - nt), (0, 0), (0, 0), (0, 0)))
    new_rows = new_rows.reshape(nb, n_new_blk, SEL_BLOCK, 2, NSA_KV_HEADS, HEAD_DIM).astype(pool.dtype)
    blk_new = new_rows[bi, jnp.clip(idx - n_past_blk, 0, n_new_blk - 1), :, :, hi, :]
    blk = jnp.where((idx >= n_past_blk)[..., None, None, None], blk_new, blk_past)
    o_s = nsa_selected(q, pos, idx, blk[..., 0, :], blk[..., 1, :])
    win = jnp.concatenate([state_win_kv, kv[:, :, 2].astype(state_win_kv.dtype)], axis=1)
    wb = state_win_kv.shape[1]
    kpos = past - wb + jnp.arange(wb + nt)
    o_w = nsa_window(q, pos, win[:, :, 0], win[:, :, 1], kpos)
    o = nsa_combine(gn, o_c, o_s, o_w).reshape(nb, nt, NSA_WIDTH)
    return o, win[:, wb + nt - min(WINDOW, wb + nt):]


def setup_inputs(seed: int = 0) -> dict:
    key = jax.random.key(seed)
    ks = iter(jax.random.split(key, 48))
    nrm = lambda shape, scale: scale * jax.random.normal(next(ks), shape, jnp.float32)
    n_pages = PAST_LEN // PAGE_SIZE
    n_used = DEC_BATCH * n_pages
    n_pool = n_used + max(1, n_used // 4)
    win_buf = min(WINDOW, PAST_LEN)
    D, F, W = D_MODEL, FFN_HIDDEN, RWKV_WIDTH
    inp = {}
    inp['x_prompt'] = nrm((BATCH, SEQ, D), 1.0)
    inp['x_sample'] = nrm((DEC_BATCH, DEC_SEQ, D), 1.0)
    inp['cache_cmp_kv'] = nrm((n_pool, PAGE_SIZE, 2, NSA_KV_HEADS, HEAD_DIM), 1.0)
    inp['cache_sel_kv'] = nrm((n_pool, PAGE_SIZE, 2, NSA_KV_HEADS, HEAD_DIM), 1.0)
    inp['page_table'] = jax.random.permutation(next(ks), n_pool)[:n_used].reshape(DEC_BATCH, n_pages).astype(jnp.int32)
    inp['state_win_kv'] = nrm((DEC_BATCH, win_buf, 2, NSA_KV_HEADS, HEAD_DIM), 1.0)
    inp['state_rwkv'] = nrm((DEC_BATCH, RWKV_HEADS, RWKV_HEAD_SIZE, RWKV_HEAD_SIZE), 0.5)
    inp['state_rwkv_shift'] = nrm((DEC_BATCH, RWKV_COLS), 1.0)
    inp['ln1_g'] = 1.0 + nrm((D,), 0.02)
    inp['ln1_b'] = nrm((D,), 0.02)
    inp['ffn1_w_gu'] = nrm((D, 2 * F), D ** -0.5)
    inp['ffn1_w_down'] = nrm((F, D), F ** -0.5 * DEEPNORM_BETA)
    inp['w_in'] = nrm((D, IN_COLS), D ** -0.5)
    inp['rwkv_mu'] = jax.random.uniform(next(ks), (RWKV_COLS,), jnp.float32)
    inp['rwkv_w0'] = jax.random.uniform(next(ks), (W,), jnp.float32, minval=-2.0, maxval=0.0)
    inp['rwkv_w2'] = nrm((DECAY_LORA, W), 0.5 * DECAY_LORA ** -0.5)
    inp['rwkv_a0'] = nrm((W,), 0.5)
    inp['rwkv_a2'] = nrm((AAA_LORA, W), AAA_LORA ** -0.5)
    inp['rwkv_g2'] = nrm((GATE_LORA, W), GATE_LORA ** -0.5)
    inp['rwkv_k_k'] = 0.85 + nrm((W,), 0.05)
    inp['rwkv_k_a'] = 1.0 + nrm((W,), 0.05)
    inp['rwkv_r_k'] = nrm((RWKV_HEADS, RWKV_HEAD_SIZE), 0.1)
    inp['rwkv_ln_w'] = 1.0 + nrm((W,), 0.02)
    inp['rwkv_ln_b'] = nrm((W,), 0.02)
    inp['w_up_a'] = nrm((W, D), W ** -0.5 * DEEPNORM_BETA)
    inp['cmp_pe'] = nrm((2, CMP_BLOCK, HEAD_DIM), 0.1)
    inp['cmp_wa'] = nrm((2, CMP_BLOCK, CMP_RANK), CMP_BLOCK ** -0.5)
    inp['cmp_wb'] = nrm((2, CMP_RANK, HEAD_DIM, HEAD_DIM), (CMP_RANK * HEAD_DIM) ** -0.5)
    inp['w_up_b'] = nrm((NSA_WIDTH, D), NSA_WIDTH ** -0.5 * DEEPNORM_BETA)
    inp['w_o'] = nrm((D, D), D ** -0.5 * DEEPNORM_BETA)
    inp['ln2_g'] = 1.0 + nrm((D,), 0.02)
    inp['ln2_b'] = nrm((D,), 0.02)
    inp['ffn2_w_gu'] = nrm((D, 2 * F), D ** -0.5)
    inp['ffn2_w_down'] = nrm((F, D), F ** -0.5 * DEEPNORM_BETA)
    inp['ln3_g'] = 1.0 + nrm((D,), 0.02)
    inp['ln3_b'] = nrm((D,), 0.02)
    return inp


def reference(x_prompt, x_sample, cache_cmp_kv, cache_sel_kv, page_table, state_win_kv, state_rwkv,
              state_rwkv_shift, ln1_g, ln1_b, ffn1_w_gu, ffn1_w_down, w_in, rwkv_mu, rwkv_w0, rwkv_w2,
              rwkv_a0, rwkv_a2, rwkv_g2, rwkv_k_k, rwkv_k_a, rwkv_r_k, rwkv_ln_w, rwkv_ln_b, w_up_a,
              cmp_pe, cmp_wa, cmp_wb, w_up_b, w_o, ln2_g, ln2_b, ffn2_w_gu, ffn2_w_down, ln3_g, ln3_b):
    rwkv_params = (rwkv_mu, rwkv_w0, rwkv_w2, rwkv_a0, rwkv_a2, rwkv_g2, rwkv_k_k, rwkv_k_a, rwkv_r_k,
                   rwkv_ln_w, rwkv_ln_b, w_up_a)
    bp, tp = x_prompt.shape[:2]
    xp, xs = x_prompt, x_sample
    for _ in range(DEPTH):
        hp = half_ffn_post_norm(xp, ffn1_w_gu, ffn1_w_down, ln1_g, ln1_b)
        ua_p, q_p, kv_p, gn_p, gm_p = split_projection(hp @ w_in)
        ya_p, s_p, shift_p = rwkv7_time_mix(
            ua_p, jnp.zeros((bp, RWKV_COLS), ua_p.dtype),
            jnp.zeros((bp, RWKV_HEADS, RWKV_HEAD_SIZE, RWKV_HEAD_SIZE), jnp.float32), *rwkv_params)
        yb_p = nsa_prompt(q_p, kv_p, gn_p, cmp_pe, cmp_wa, cmp_wb) @ w_up_b
        hp = gated_merge_post_norm(hp, ya_p, yb_p, gm_p, w_o, ln2_g, ln2_b)
        xp = half_ffn_post_norm(hp, ffn2_w_gu, ffn2_w_down, ln3_g, ln3_b)
        hs = half_ffn_post_norm(xs, ffn1_w_gu, ffn1_w_down, ln1_g, ln1_b)
        ua_s, q_s, kv_s, gn_s, gm_s = split_projection(hs @ w_in)
        ya_s, s_s, shift_s = rwkv7_time_mix(ua_s, state_rwkv_shift, state_rwkv, *rwkv_params)
        o_s, win_s = nsa_sample(q_s, kv_s, gn_s, cache_cmp_kv, cache_sel_kv, page_table, state_win_kv,
                                cmp_pe, cmp_wa, cmp_wb)
        yb_s = o_s @ w_up_b
        hs = gated_merge_post_norm(hs, ya_s, yb_s, gm_s, w_o, ln2_g, ln2_b)
        xs = half_ffn_post_norm(hs, ffn2_w_gu, ffn2_w_down, ln3_g, ln3_b)
    cd, sd, wd = cache_cmp_kv.dtype, cache_sel_kv.dtype, state_win_kv.dtype
    rd, hd = state_rwkv.dtype, state_rwkv_shift.dtype
    return (xp, xs,
            kv_p[:, :, 0].astype(cd), kv_p[:, :, 1].astype(sd),
            kv_p[:, tp - min(WINDOW, tp):, 2].astype(wd), s_p.astype(rd), shift_p.astype(hd),
            kv_s[:, :, 0].astype(cd), kv_s[:, :, 1].astype(sd),
            win_s.astype(wd), s_s.astype(rd), shift_s.astype(hd))
```

```python
import functools
import math

import jax
import jax.numpy as jnp
from jax import lax
from jax.experimental import pallas as pl
from jax.experimental.pallas import tpu as pltpu

F32 = jnp.float32
BF16 = jnp.bfloat16

D_MODEL = 2048
RWKV_WIDTH = D_MODEL // 2
RWKV_HEAD_SIZE = 64
RWKV_HEADS = RWKV_WIDTH // RWKV_HEAD_SIZE
DECAY_LORA = 64
AAA_LORA = 64
GATE_LORA = 160
RWKV_GN_EPS = 64e-5
NSA_WIDTH = D_MODEL // 2
HEAD_DIM = 64
NSA_HEADS = NSA_WIDTH // HEAD_DIM
NSA_KV_HEADS = 4
NSA_GROUP = NSA_HEADS // NSA_KV_HEADS
CMP_STRIDE = 16
CMP_BLOCK = 2 * CMP_STRIDE
CMP_RANK = 4
SEL_BLOCK = 64
N_SEL = 16
WINDOW = 512
Q_BLOCK = 128
PAGE_SIZE = 128
ATTN_SCALE = HEAD_DIM ** -0.5
FFN_HIDDEN = 256 * math.ceil(8 * D_MODEL / 3 / 256)
DEPTH = 1
DEEPNORM_ALPHA = (2 * DEPTH) ** 0.25
LN_EPS = 1e-5
NEG_INF = -1e30
FORCE = 1e9
RWKV_COLS = 3 * RWKV_WIDTH + DECAY_LORA + AAA_LORA + GATE_LORA
Q_COLS = NSA_HEADS * HEAD_DIM
KV_COLS = 3 * 2 * NSA_KV_HEADS * HEAD_DIM
NSA_GATE_COLS = 3 * NSA_HEADS
MERGE_GATE_COLS = 2 * D_MODEL

LANES = 128
SUBLANES = 8
VMEM_LIMIT_BYTES = 56 * 1024 * 1024

UA_PAD = 3584
GN_PAD = 128
OFF_GM = 0
OFF_Q = OFF_GM + MERGE_GATE_COLS
OFF_KV = OFF_Q + Q_COLS
OFF_UA = 2 * UA_PAD
OFF_GN = OFF_UA + UA_PAD
IN_PAD = OFF_GN + GN_PAD


def pad_in_cols(w):
    o = 0
    ua = w[..., o:o + RWKV_COLS]; o += RWKV_COLS
    q = w[..., o:o + Q_COLS]; o += Q_COLS
    kv = w[..., o:o + KV_COLS]; o += KV_COLS
    gn = w[..., o:o + NSA_GATE_COLS]; o += NSA_GATE_COLS
    gm = w[..., o:o + MERGE_GATE_COLS]
    z = lambda n: jnp.zeros(w.shape[:-1] + (n,), w.dtype)
    return jnp.concatenate([gm, q, kv, z(OFF_UA - (OFF_KV + KV_COLS)), ua, z(UA_PAD - RWKV_COLS), gn,
                            z(GN_PAD - NSA_GATE_COLS)], axis=-1)


def _cparams(*sem):
    return pltpu.CompilerParams(dimension_semantics=sem, vmem_limit_bytes=VMEM_LIMIT_BYTES)


def _layer_norm(y, g, b):
    mu = jnp.mean(y, axis=-1, keepdims=True)
    d = y - mu
    var = jnp.mean(d * d, axis=-1, keepdims=True)
    return d * lax.rsqrt(var + LN_EPS) * g + b


def _ffn_ln_kernel(x_ref, wg_ref, wu_ref, wd_ref, g_ref, b_ref, o_ref, ob_ref, xb_ref, acc_ref, *, n_chunks):
    j = pl.program_id(1)

    @pl.when(j == 0)
    def _():
        xb_ref[...] = x_ref[...].astype(BF16)
        acc_ref[...] = jnp.zeros_like(acc_ref)

    xb = xb_ref[...]
    gate = jnp.dot(xb, wg_ref[...], preferred_element_type=F32)
    up = jnp.dot(xb, wu_ref[...], preferred_element_type=F32)
    act = (gate * jax.nn.sigmoid(gate)) * up
    acc_ref[...] += jnp.dot(act.astype(BF16), wd_ref[...], preferred_element_type=F32)

    @pl.when(j == n_chunks - 1)
    def _():
        y = DEEPNORM_ALPHA * x_ref[...] + 0.5 * acc_ref[...]
        o = _layer_norm(y, g_ref[...], b_ref[...])
        o_ref[...] = o
        ob_ref[...] = o.astype(BF16)


def ffn_ln(x, w_gu_bf, w_down_bf, g, b, *, tm, tf=512):
    m, d = x.shape
    f = w_down_bf.shape[0]
    n_chunks = f // tf
    assert m % tm == 0 and f % tf == 0
    return pl.pallas_call(
        functools.partial(_ffn_ln_kernel, n_chunks=n_chunks),
        grid=(m // tm, n_chunks),
        in_specs=[
            pl.BlockSpec((tm, d), lambda i, j: (i, 0)),
            pl.BlockSpec((d, tf), lambda i, j: (0, j)),
            pl.BlockSpec((d, tf), lambda i, j: (0, j + n_chunks)),
            pl.BlockSpec((tf, d), lambda i, j: (j, 0)),
            pl.BlockSpec((1, d), lambda i, j: (0, 0)),
            pl.BlockSpec((1, d), lambda i, j: (0, 0)),
        ],
        out_specs=[
            pl.BlockSpec((tm, d), lambda i, j: (i, 0)),
            pl.BlockSpec((tm, d), lambda i, j: (i, 0)),
        ],
        out_shape=[jax.ShapeDtypeStruct((m, d), F32), jax.ShapeDtypeStruct((m, d), BF16)],
        scratch_shapes=[pltpu.VMEM((tm, d), BF16), pltpu.VMEM((tm, d), F32)],
        compiler_params=_cparams("parallel", "arbitrary"),
        name="ffn_ln",
    )(x, w_gu_bf, w_gu_bf, w_down_bf, g.reshape(1, d), b.reshape(1, d))


def _matmul_kernel(x_ref, w_ref, o_ref):
    o_ref[...] = jnp.dot(x_ref[...], w_ref[...], preferred_element_type=F32)


def matmul(x_bf, w_bf, *, tm, tn):
    m, k = x_bf.shape
    n = w_bf.shape[1]
    assert m % tm == 0 and n % tn == 0
    return pl.pallas_call(
        _matmul_kernel,
        grid=(m // tm, n // tn),
        in_specs=[pl.BlockSpec((tm, k), lambda i, j: (i, 0)), pl.BlockSpec((k, tn), lambda i, j: (0, j))],
        out_specs=pl.BlockSpec((tm, tn), lambda i, j: (i, j)),
        out_shape=jax.ShapeDtypeStruct((m, n), F32),
        compiler_params=_cparams("parallel", "arbitrary"),
        name="proj_matmul",
    )(x_bf, w_bf)


def _split2(x):
    hi = x.astype(BF16)
    lo = (x - hi.astype(F32)).astype(BF16)
    return hi, lo


def _split3(x):
    hi = x.astype(BF16)
    r1 = x - hi.astype(F32)
    mid = r1.astype(BF16)
    lo = (r1 - mid.astype(F32)).astype(BF16)
    return hi, mid, lo


def _head_sum(x, bo):
    outs = []
    for c in range(x.shape[1] // LANES):
        hi, mid, lo = _split3(x[:, c * LANES:(c + 1) * LANES])
        s = jnp.dot(hi, bo, preferred_element_type=F32)
        s += jnp.dot(mid, bo, preferred_element_type=F32)
        s += jnp.dot(lo, bo, preferred_element_type=F32)
        outs.append(s)
    return jnp.concatenate(outs, axis=1)


def _rwkv_prep_kernel(ua_ref, first_ref, mu_ref, w0_ref, a0_ref, kk_ref, ka_ref, w2_ref, a2_ref, g2_ref, bo_ref,
                      r_out, w_out, kx_out, v_out, kkn_out, b_out, g_out, vt_out, *, period):
    tm = ua_ref.shape[0]
    W = RWKV_WIDTH
    uf = ua_ref[...]
    rolled = pltpu.roll(uf, 1, axis=0)
    row = lax.broadcasted_iota(jnp.int32, (tm, 1), 0)
    if period >= tm:
        first = jnp.broadcast_to(first_ref[0:1, :], uf.shape)
        is_first = row == 0
    else:
        first = first_ref[...]
        is_first = (row % period) == 0
    prev = jnp.where(is_first, first, rolled)
    um = uf + (prev - uf) * mu_ref[...]
    r = um[:, 0:W]
    k = um[:, W:2 * W]
    v = um[:, 2 * W:3 * W]
    tail = um[:, 3 * W:UA_PAD]
    lw = jnp.dot(jnp.tanh(tail).astype(BF16), w2_ref[...], preferred_element_type=F32)
    z = -(w0_ref[...] + lw)
    softplus = jnp.maximum(z, 0.0) + jnp.log1p(jnp.exp(-jnp.abs(z)))
    w_log = -softplus - 0.5
    decay = jnp.exp(-jnp.exp(w_log))
    a = jax.nn.sigmoid(a0_ref[...] + jnp.dot(tail.astype(BF16), a2_ref[...], preferred_element_type=F32))
    g = jnp.dot(jax.nn.sigmoid(tail).astype(BF16), g2_ref[...], preferred_element_type=F32)
    kk = k * kk_ref[...]
    n2 = _head_sum(kk * kk, bo_ref[...])
    kk = kk / jnp.maximum(jnp.sqrt(n2), 1e-12)
    kx = k * (1.0 + (a - 1.0) * ka_ref[...])
    r_out[...] = r
    w_out[...] = decay
    kx_out[...] = kx
    v_out[...] = v
    kkn_out[...] = kk
    b_out[...] = kk * a
    g_out[...] = g
    vt_out[...] = v.T


def rwkv_prep(u, first, consts, *, m0, m, tm, period):
    mu, w0, a0, k_k, k_a, w2p, a2p, g2p, bo = consts
    W = RWKV_WIDTH
    nt = m // tm
    b0 = m0 // tm
    assert m % tm == 0 and m0 % tm == 0
    first_rows = first.shape[0] // nt
    row_spec = lambda width: pl.BlockSpec((1, width), lambda i: (0, 0))
    full = lambda a: pl.BlockSpec(a.shape, lambda i: (0, 0))
    out_tile = pl.BlockSpec((tm, W), lambda i: (i, 0))
    return pl.pallas_call(
        functools.partial(_rwkv_prep_kernel, period=period),
        grid=(nt,),
        in_specs=[
            pl.BlockSpec((tm, UA_PAD), lambda i: (i + b0, OFF_UA // UA_PAD)),
            pl.BlockSpec((first_rows, UA_PAD), lambda i: (i, 0)),
            row_spec(UA_PAD), row_spec(W), row_spec(W), row_spec(W), row_spec(W),
            full(w2p), full(a2p), full(g2p), full(bo),
        ],
        out_specs=[out_tile] * 7 + [pl.BlockSpec((W, tm), lambda i: (0, i))],
        out_shape=[jax.ShapeDtypeStruct((m, W), F32)] * 7 + [jax.ShapeDtypeStruct((W, m), F32)],
        compiler_params=_cparams("parallel"),
        name="rwkv_prep",
    )(u, first, mu, w0, a0, k_k, k_a, w2p, a2p, g2p, bo)


def _rwkv_scan_kernel(r_ref, w_ref, kx_ref, kk_ref, b_ref, vt_ref, s0_ref, bo2_ref, y_ref, sout_ref,
                      s_scr, vhl_scr, *, tc, n_tb):
    tb = pl.program_id(1)

    @pl.when(tb == 0)
    def _():
        s_scr[...] = s0_ref[0]

    vt = vt_ref[0]
    vh, vl = _split2(vt)
    vhl_scr[0] = vh
    vhl_scr[1] = vl
    left = lax.broadcasted_iota(jnp.int32, (64, LANES), 1) < 64
    left_bf = lax.broadcasted_iota(jnp.int32, (64, LANES), 1) < 64
    tau = lax.broadcasted_iota(jnp.int32, (tc, LANES), 0)
    bo2 = bo2_ref[...]
    nt_dims = (((1,), (1,)), ((), ()))
    zero_bf = jnp.zeros((64, LANES), BF16)

    sub = lax.broadcasted_iota(jnp.int32, (SUBLANES, LANES), 0)

    def group(gi, carry):
        base = pl.multiple_of(gi * SUBLANES, SUBLANES)
        for p in range(RWKV_HEADS // 2):
            cs = slice(p * LANES, (p + 1) * LANES)
            w8 = w_ref[0, pl.ds(base, SUBLANES), cs]
            kx8 = kx_ref[0, pl.ds(base, SUBLANES), cs]
            kk8 = kk_ref[0, pl.ds(base, SUBLANES), cs]
            b8 = b_ref[0, pl.ds(base, SUBLANES), cs]
            r8 = r_ref[0, pl.ds(base, SUBLANES), cs]
            y8 = jnp.zeros((SUBLANES, LANES), F32)
            s_cur = s_scr[p]
            for j in range(SUBLANES):
                e_t = (tau == base + j).astype(BF16)
                xh, xl = _split2(s_cur * kk8[j:j + 1])
                sa = jnp.dot(jnp.concatenate([xh, xl], axis=1), bo2, preferred_element_type=F32)
                vb = jnp.dot(vhl_scr[0, cs, :], e_t, preferred_element_type=F32)
                vb += jnp.dot(vhl_scr[1, cs, :], e_t, preferred_element_type=F32)
                vcol = jnp.where(left, vb[0:64], vb[64:128])
                s_cur = s_cur * w8[j:j + 1] - sa * b8[j:j + 1] + vcol * kx8[j:j + 1]
                sh, sl = _split2(s_cur)
                sh2 = jnp.concatenate([jnp.where(left_bf, sh, zero_bf), jnp.where(left_bf, zero_bf, sh)], axis=0)
                sl2 = jnp.concatenate([jnp.where(left_bf, sl, zero_bf), jnp.where(left_bf, zero_bf, sl)], axis=0)
                rh, rl = _split2(jnp.broadcast_to(r8[j:j + 1], (SUBLANES, LANES)))
                y2 = lax.dot_general(jnp.concatenate([rh, rl], axis=0), sh2, nt_dims, preferred_element_type=F32)
                y3 = lax.dot_general(rh, sl2, nt_dims, preferred_element_type=F32)
                yy = y2[0:SUBLANES] + y2[SUBLANES:2 * SUBLANES] + y3
                y8 = jnp.where(sub == j, yy, y8)
            s_scr[p] = s_cur
            y_ref[0, pl.ds(base, SUBLANES), cs] = y8
        return carry

    lax.fori_loop(0, tc // SUBLANES, group, 0)

    @pl.when(tb == n_tb - 1)
    def _():
        sout_ref[0] = s_scr[...]


def rwkv_scan(r, w, kx, kk, b, vt, s0_slabs, bo2, *, tc):
    bsz, t, W = r.shape
    n_tb = t // tc
    assert t % tc == 0
    seq = pl.BlockSpec((1, tc, W), lambda bi, ti: (bi, ti, 0))
    st = pl.BlockSpec((1, RWKV_HEADS // 2, 64, LANES), lambda bi, ti: (bi, 0, 0, 0))
    return pl.pallas_call(
        functools.partial(_rwkv_scan_kernel, tc=tc, n_tb=n_tb),
        grid=(bsz, n_tb),
        in_specs=[seq, seq, seq, seq, seq,
                  pl.BlockSpec((1, W, tc), lambda bi, ti: (bi, 0, ti)),
                  st,
                  pl.BlockSpec(bo2.shape, lambda bi, ti: (0, 0))],
        out_specs=[seq, st],
        out_shape=[jax.ShapeDtypeStruct((bsz, t, W), F32),
                   jax.ShapeDtypeStruct((bsz, RWKV_HEADS // 2, 64, LANES), F32)],
        scratch_shapes=[pltpu.VMEM((RWKV_HEADS // 2, 64, LANES), F32), pltpu.VMEM((2, W, tc), BF16)],
        compiler_params=_cparams("parallel", "arbitrary"),
        name="rwkv_scan",
    )(r, w, kx, kk, b, vt, s0_slabs, bo2)


def _rwkv_post_kernel(y_ref, r_ref, kx_ref, v_ref, g_ref, rk_ref, lnw_ref, lnb_ref, bo_ref, o_ref):
    bo = bo_ref[...]
    y = y_ref[...]
    inv_n = 1.0 / RWKV_HEAD_SIZE
    ym = _head_sum(y, bo) * inv_n
    d = y - ym
    yv = _head_sum(d * d, bo) * inv_n
    yn = d * lax.rsqrt(yv + RWKV_GN_EPS) * lnw_ref[...] + lnb_ref[...]
    bonus = _head_sum(r_ref[...] * kx_ref[...] * rk_ref[...], bo) * v_ref[...]
    o_ref[...] = ((yn + bonus) * g_ref[...]).astype(BF16)


def rwkv_post(y, r, kx, v, g, r_k, ln_w, ln_b, bo, *, tm):
    m, W = y.shape
    tile = pl.BlockSpec((tm, W), lambda i: (i, 0))
    row = pl.BlockSpec((1, W), lambda i: (0, 0))
    return pl.pallas_call(
        _rwkv_post_kernel,
        grid=(m // tm,),
        in_specs=[tile] * 5 + [row] * 3 + [pl.BlockSpec(bo.shape, lambda i: (0, 0))],
        out_specs=tile,
        out_shape=jax.ShapeDtypeStruct((m, W), BF16),
        compiler_params=_cparams("parallel"),
        name="rwkv_post",
    )(y, r, kx, v, g, r_k, ln_w, ln_b, bo)


def rwkv_consts(mu, w0, w2, a0, a2, g2, k_k, k_a):
    W = RWKV_WIDTH
    tail = UA_PAD - 3 * W
    mu_p = jnp.zeros((1, UA_PAD), F32).at[0, :RWKV_COLS].set(mu.astype(F32))
    o_a = DECAY_LORA
    o_g = DECAY_LORA + AAA_LORA
    w2p = jnp.zeros((tail, W), BF16).at[0:o_a].set(w2.astype(BF16))
    a2p = jnp.zeros((tail, W), BF16).at[o_a:o_g].set(a2.astype(BF16))
    g2p = jnp.zeros((tail, W), BF16).at[o_g:o_g + GATE_LORA].set(g2.astype(BF16))
    half = jnp.arange(LANES) // RWKV_HEAD_SIZE
    bo = (half[:, None] == half[None, :]).astype(BF16)
    row = lambda p: p.astype(F32).reshape(1, W)
    return (mu_p, row(w0), row(a0), row(k_k), row(k_a), w2p, a2p, g2p, bo)


def _state_to_slabs(s):
    bsz = s.shape[0]
    hp = RWKV_HEADS // 2
    return s.reshape(bsz, hp, 2, 64, 64).transpose(0, 1, 3, 2, 4).reshape(bsz, hp, 64, LANES)


def _slabs_to_state(s):
    bsz = s.shape[0]
    hp = RWKV_HEADS // 2
    return s.reshape(bsz, hp, 64, 2, 64).transpose(0, 1, 3, 2, 4).reshape(bsz, RWKV_HEADS, 64, 64)


SCAN_TC_PROMPT = 128
SCAN_TC_SAMPLE = 16
PREP_TM = 256


def rwkv_time_mix(u, m0, m, bsz, t, shift_state, s0, consts, r_k, ln_w, ln_b, *, prompt):
    W = RWKV_WIDTH
    bo = consts[-1]
    bo2 = jnp.concatenate([bo, bo], axis=0)
    if prompt:
        tm = min(PREP_TM, m)
        nt = m // tm
        prev_rows = u[m0 + tm - 1:m0 + m - 1:tm, OFF_UA:OFF_UA + UA_PAD]
        rows = jnp.concatenate([jnp.zeros((1, UA_PAD), F32), prev_rows], axis=0)
        first = jnp.zeros((nt, SUBLANES, UA_PAD), F32).at[:, 0].set(rows).reshape(nt * SUBLANES, UA_PAD)
        period = tm
    else:
        tm = m
        sp = jnp.zeros((bsz, UA_PAD), F32).at[:, :RWKV_COLS].set(shift_state.astype(F32))
        first = jnp.repeat(sp, t, axis=0)
        period = t
    r, w, kx, v, kk, b, g, vt = rwkv_prep(u, first, consts, m0=m0, m=m, tm=tm, period=period)
    if prompt:
        tc = min(SCAN_TC_PROMPT, t)
        seq = lambda a: a.reshape(bsz, t, W)
        vt3 = vt.reshape(1, W, m)
        s0_slabs = jnp.zeros((bsz, RWKV_HEADS // 2, 64, LANES), F32)
        tpad = t
    else:
        tc = SCAN_TC_SAMPLE
        tpad = tc
        padt = lambda a, val: jnp.pad(a.reshape(bsz, t, W), ((0, 0), (0, tpad - t), (0, 0)), constant_values=val)
        seq = lambda a: padt(a, 0.0)
        vt3 = jnp.pad(vt.reshape(W, bsz, t).transpose(1, 0, 2), ((0, 0), (0, 0), (0, tpad - t)))
        s0_slabs = _state_to_slabs(s0.astype(F32))
    w3 = seq(w) if prompt else jnp.pad(w.reshape(bsz, t, W), ((0, 0), (0, tpad - t), (0, 0)), constant_values=1.0)
    y3, s_fin = rwkv_scan(seq(r), w3, seq(kx), seq(kk), seq(b), vt3, s0_slabs, bo2, tc=tc)
    y = y3[:, :t].reshape(m, W)
    za = rwkv_post(y, r, kx, v, g, r_k.astype(F32).reshape(1, W), ln_w.astype(F32).reshape(1, W),
                   ln_b.astype(F32).reshape(1, W), bo, tm=tm)
    return za, _slabs_to_state(s_fin)


KV_ROW = 2 * NSA_KV_HEADS * HEAD_DIM
CHUNKS_PER_PAGE = PAGE_SIZE // CMP_STRIDE
PROJ_W = CMP_RANK * KV_ROW


def _cmp_proj_kernel(pt_ref, x_ref, waf_ref, was_ref, f_ref, s_ref):
    del pt_ref
    for r in range(CMP_RANK):
        f = jnp.zeros((CHUNKS_PER_PAGE, KV_ROW), F32)
        s = jnp.zeros((CHUNKS_PER_PAGE, KV_ROW), F32)
        for l in range(CMP_STRIDE):
            xl = x_ref[0, :, l * KV_ROW:(l + 1) * KV_ROW]
            f += xl * waf_ref[r * CMP_STRIDE + l:r * CMP_STRIDE + l + 1, :]
            s += xl * was_ref[r * CMP_STRIDE + l:r * CMP_STRIDE + l + 1, :]
        f_ref[0, :, r * KV_ROW:(r + 1) * KV_ROW] = f
        s_ref[0, :, r * KV_ROW:(r + 1) * KV_ROW] = s


def cmp_project(pages, page_table_flat, bsz, n_pages, waf, was):
    n_ch = n_pages * CHUNKS_PER_PAGE
    out = jax.ShapeDtypeStruct((bsz, n_ch, PROJ_W), F32)
    o_spec = pl.BlockSpec((1, CHUNKS_PER_PAGE, PROJ_W), lambda b, p, pt: (b, p, 0))
    w_spec = pl.BlockSpec(waf.shape, lambda b, p, pt: (0, 0))
    return pl.pallas_call(
        _cmp_proj_kernel,
        grid_spec=pltpu.PrefetchScalarGridSpec(
            num_scalar_prefetch=1,
            grid=(bsz, n_pages),
            in_specs=[pl.BlockSpec((1, CHUNKS_PER_PAGE, CMP_STRIDE * KV_ROW),
                                   lambda b, p, pt: (pt[b * n_pages + p], 0, 0)),
                      w_spec, w_spec],
            out_specs=[o_spec, o_spec],
        ),
        out_shape=[out, out],
        compiler_params=_cparams("parallel", "arbitrary"),
        name="cmp_project",
    )(page_table_flat, pages, waf, was)


def _cmp_mix_kernel(f_ref, s_ref, waf_ref, was_ref, pex_ref, wb_ref, o_ref):
    n_ch = f_ref.shape[1]
    acc = jnp.zeros((n_ch, KV_ROW), F32)
    for r in range(CMP_RANK):
        rs = slice(r * CMP_STRIDE, (r + 1) * CMP_STRIDE)
        pe = jnp.sum(waf_ref[rs, :] * pex_ref[0:CMP_STRIDE, :], axis=0, keepdims=True)
        pe += jnp.sum(was_ref[rs, :] * pex_ref[CMP_STRIDE:CMP_BLOCK, :], axis=0, keepdims=True)
        cs = slice(r * KV_ROW, (r + 1) * KV_ROW)
        nxt = pltpu.roll(s_ref[0, :, cs], n_ch - 1, axis=0)
        hid = jax.nn.gelu(f_ref[0, :, cs] + nxt + pe)
        acc += jnp.dot(hid.astype(BF16), wb_ref[r], preferred_element_type=F32)
    o_ref[0] = acc.astype(BF16)


def cmp_mix(f, s, waf, was, pex, wb_bd):
    bsz, n_ch, _ = f.shape
    io = pl.BlockSpec((1, n_ch, PROJ_W), lambda b: (b, 0, 0))
    full2 = lambda a: pl.BlockSpec(a.shape, lambda b: (0, 0))
    return pl.pallas_call(
        _cmp_mix_kernel,
        grid=(bsz,),
        in_specs=[io, io, full2(waf), full2(was), full2(pex), pl.BlockSpec(wb_bd.shape, lambda b: (0, 0, 0))],
        out_specs=pl.BlockSpec((1, n_ch, KV_ROW), lambda b: (b, 0, 0)),
        out_shape=jax.ShapeDtypeStruct((bsz, n_ch, KV_ROW), BF16),
        compiler_params=_cparams("parallel"),
        name="cmp_mix",
    )(f, s, waf, was, pex, wb_bd)


def cmp_consts(cmp_pe, cmp_wa, cmp_wb):
    hd = NSA_KV_HEADS * HEAD_DIM
    wa_cols = jnp.repeat(cmp_wa.astype(F32).transpose(2, 1, 0), hd, axis=2)
    waf = wa_cols[:, :CMP_STRIDE].reshape(CMP_RANK * CMP_STRIDE, KV_ROW)
    was = wa_cols[:, CMP_STRIDE:].reshape(CMP_RANK * CMP_STRIDE, KV_ROW)
    pex = jnp.broadcast_to(cmp_pe.astype(F32).transpose(1, 0, 2)[:, :, None, :],
                           (CMP_BLOCK, 2, NSA_KV_HEADS, HEAD_DIM)).reshape(CMP_BLOCK, KV_ROW)
    eye = jnp.eye(NSA_KV_HEADS, dtype=F32)
    wb = cmp_wb.astype(F32)
    bd = jnp.einsum('srde,hg,st->rshdtge', wb, eye, jnp.eye(2, dtype=F32)).reshape(CMP_RANK, KV_ROW, KV_ROW)
    return waf, was, pex, bd.astype(BF16)


def compress(pages, page_table_flat, bsz, n_pages, consts):
    waf, was, pex, wb_bd = consts
    f, s = cmp_project(pages, page_table_flat, bsz, n_pages, waf, was)
    kvc = cmp_mix(f, s, waf, was, pex, wb_bd)
    ratio = SEL_BLOCK // CMP_STRIDE
    n_ch = LANES * ratio
    kvc = jnp.pad(kvc, ((0, 0), (0, n_ch - kvc.shape[1]), (0, 0)))
    return kvc.reshape(bsz, LANES, ratio, KV_ROW).transpose(0, 2, 1, 3).reshape(bsz, n_ch, KV_ROW)


NT_DIMS = (((1,), (1,)), ((), ()))
HALF = LANES // 2
KEY_TILE = 512
WIN_SPAN = WINDOW + Q_BLOCK


def _masked_softmax_rows(s, ok):
    x = jnp.where(ok, s, NEG_INF)
    m = jnp.max(x, axis=-1, keepdims=True)
    e = jnp.exp(x - m)
    l = jnp.sum(e, axis=-1, keepdims=True)
    return jnp.where(ok, e / l, 0.0)


def _top_blocks(score, n_keep):
    lane = lax.broadcasted_iota(jnp.int32, score.shape, 1)

    def body(d, cnt):
        rolled = pltpu.roll(score, d, axis=1)
        beats = (rolled > score) | ((rolled == score) & (lane >= d))
        return cnt + jnp.where(beats, 1.0, 0.0)

    cnt = lax.fori_loop(1, LANES, body, jnp.zeros(score.shape, F32))
    return cnt < n_keep


def _stack_group_queries(q_f32, h, khalf):
    lane = lax.broadcasted_iota(jnp.int32, (Q_BLOCK, LANES), 1)
    keep = (lane // HALF) == khalf
    parts = []
    for g in range(NSA_GROUP):
        head = h * NSA_GROUP + g
        slab = q_f32[:, (head // 2) * LANES:(head // 2 + 1) * LANES]
        if head % 2 != khalf:
            slab = pltpu.roll(slab, HALF, axis=1)
        parts.append(jnp.where(keep, slab, 0.0).astype(BF16))
    return jnp.concatenate(parts, axis=0)


def _nsa_prompt_kernel(q_ref, gn_ref, kc_ref, vc_ref, ks_ref, vs_ref, kw_ref, vw_ref, o_ref):
    i = pl.program_id(0)
    start = i * Q_BLOCK
    q = q_ref[...]
    gates = jax.nn.sigmoid(gn_ref[...])
    rows = NSA_GROUP * Q_BLOCK
    pos1 = start + lax.broadcasted_iota(jnp.int32, (Q_BLOCK, 1), 0)
    pos4 = jnp.concatenate([pos1] * NSA_GROUP, axis=0)
    n_cmp = kc_ref.shape[0]
    col = lax.broadcasted_iota(jnp.int32, (1, n_cmp), 1)
    ratio = SEL_BLOCK // CMP_STRIDE
    cmp_idx = (col % LANES) * ratio + col // LANES
    cmp_ok = (cmp_idx * CMP_STRIDE + (CMP_BLOCK - 1)) <= pos4
    blk = lax.broadcasted_iota(jnp.int32, (1, LANES), 1)
    cur = pos1 // SEL_BLOCK
    forced = (blk == 0) | (blk == cur) | (blk == cur - 1)
    future = blk * SEL_BLOCK > pos1
    lane_t = lax.broadcasted_iota(jnp.int32, (1, KEY_TILE), 1)
    blk_row = lax.broadcasted_iota(jnp.int32, (LANES, 1), 0)
    lane128 = lax.broadcasted_iota(jnp.int32, (Q_BLOCK, LANES), 1)
    n_tiles = start // KEY_TILE + 1
    ws = pl.multiple_of(jnp.maximum(start - WINDOW, 0), Q_BLOCK)
    kpos_w = ws + lax.broadcasted_iota(jnp.int32, (1, WIN_SPAN), 1)
    delta = pos4 - kpos_w
    win_ok = (delta >= 0) & (delta < WINDOW)

    out_slabs = [None] * (NSA_HEADS // 2)
    for h in range(NSA_KV_HEADS):
        khalf = h % 2
        ksl = slice((h // 2) * LANES, (h // 2 + 1) * LANES)
        qh = _stack_group_queries(q, h, khalf)
        sc = lax.dot_general(qh, kc_ref[:, ksl], NT_DIMS, preferred_element_type=F32) * ATTN_SCALE
        pc = _masked_softmax_rows(sc, cmp_ok)
        o_c = jnp.dot(pc.astype(BF16), vc_ref[:, ksl], preferred_element_type=F32)
        imp = pc[0:Q_BLOCK]
        for g in range(1, NSA_GROUP):
            imp = imp + pc[g * Q_BLOCK:(g + 1) * Q_BLOCK]
        imp_blk = imp[:, 0:LANES]
        for j in range(1, ratio):
            imp_blk = imp_blk + imp[:, j * LANES:(j + 1) * LANES]
        score = jnp.where(future, -FORCE, jnp.where(forced, FORCE, imp_blk))
        sel = _top_blocks(score, N_SEL).astype(BF16)

        def sel_tile(kt, carry):
            m_i, l_i, acc = carry
            k0 = pl.multiple_of(kt * KEY_TILE, KEY_TILE)
            s = lax.dot_general(qh, ks_ref[pl.ds(k0, KEY_TILE), ksl], NT_DIMS,
                                preferred_element_type=F32) * ATTN_SCALE
            expand = (blk_row == (k0 + lane_t) // SEL_BLOCK).astype(BF16)
            chosen = jnp.dot(sel, expand, preferred_element_type=F32)
            ok = (jnp.concatenate([chosen] * NSA_GROUP, axis=0) > 0.5) & ((k0 + lane_t) <= pos4)
            x = jnp.where(ok, s, NEG_INF)
            m_new = jnp.maximum(m_i, jnp.max(x, axis=-1, keepdims=True))
            p = jnp.where(ok, jnp.exp(x - m_new), 0.0)
            scale = jnp.exp(m_i - m_new)
            l_new = scale * l_i + jnp.sum(p, axis=-1, keepdims=True)
            acc = scale * acc + jnp.dot(p.astype(BF16), vs_ref[pl.ds(k0, KEY_TILE), ksl],
                                        preferred_element_type=F32)
            return m_new, l_new, acc

        init = (jnp.full((rows, 1), NEG_INF, F32), jnp.zeros((rows, 1), F32), jnp.zeros((rows, LANES), F32))
        _, l_s, acc_s = lax.fori_loop(0, n_tiles, sel_tile, init)
        o_s = acc_s / l_s

        sw = lax.dot_general(qh, kw_ref[pl.ds(ws, WIN_SPAN), ksl], NT_DIMS, preferred_element_type=F32) * ATTN_SCALE
        pw = _masked_softmax_rows(sw, win_ok)
        o_w = jnp.dot(pw.astype(BF16), vw_ref[pl.ds(ws, WIN_SPAN), ksl], preferred_element_type=F32)

        for g in range(NSA_GROUP):
            head = h * NSA_GROUP + g
            rs = slice(g * Q_BLOCK, (g + 1) * Q_BLOCK)
            gc = gates[:, head:head + 1]
            gs = gates[:, NSA_HEADS + head:NSA_HEADS + head + 1]
            gw = gates[:, 2 * NSA_HEADS + head:2 * NSA_HEADS + head + 1]
            og = gc * o_c[rs] + gs * o_s[rs] + gw * o_w[rs]
            if head % 2 != khalf:
                og = pltpu.roll(og, HALF, axis=1)
            keep = (lane128 // HALF) == (head % 2)
            prev = out_slabs[head // 2]
            out_slabs[head // 2] = jnp.where(keep, og, 0.0 if prev is None else prev)
    for sidx in range(NSA_HEADS // 2):
        o_ref[:, sidx * LANES:(sidx + 1) * LANES] = out_slabs[sidx].astype(BF16)


def nsa_prompt(u, t, kvc, sel_kv_bf, win_kv_bf):
    n_qb = t // Q_BLOCK
    hw = NSA_KV_HEADS * HEAD_DIM
    whole = lambda a, cb: pl.BlockSpec((a.shape[0], hw), lambda i: (0, cb))
    return pl.pallas_call(
        _nsa_prompt_kernel,
        grid=(n_qb,),
        in_specs=[
            pl.BlockSpec((Q_BLOCK, Q_COLS), lambda i: (i, OFF_Q // Q_COLS)),
            pl.BlockSpec((Q_BLOCK, GN_PAD), lambda i: (i, OFF_GN // GN_PAD)),
            whole(kvc, 0), whole(kvc, 1),
            whole(sel_kv_bf, 0), whole(sel_kv_bf, 1),
            whole(win_kv_bf, 0), whole(win_kv_bf, 1),
        ],
        out_specs=pl.BlockSpec((Q_BLOCK, NSA_WIDTH), lambda i: (i, 0)),
        out_shape=jax.ShapeDtypeStruct((t, NSA_WIDTH), BF16),
        compiler_params=_cparams("arbitrary"),
        name="nsa_prompt",
    )(u, u, kvc, kvc, sel_kv_bf, sel_kv_bf, win_kv_bf, win_kv_bf)


def _tail_keys(qf, k_new, v_new, tok, m, l, acc):
    nt = k_new.shape[0]
    kn = k_new.astype(BF16).astype(F32)
    vn = v_new.astype(BF16).astype(F32)
    s = [jnp.sum(qf * kn[j:j + 1], axis=1, keepdims=True) * ATTN_SCALE for j in range(nt)]
    ok = [tok >= j for j in range(nt)]
    m_new = m
    for j in range(nt):
        m_new = jnp.maximum(m_new, jnp.where(ok[j], s[j], NEG_INF))
    scale = jnp.exp(m - m_new)
    l = scale * l
    acc = scale * acc
    for j in range(nt):
        p = jnp.where(ok[j], jnp.exp(s[j] - m_new), 0.0)
        l = l + p
        acc = acc + p * vn[j:j + 1]
    return m_new, l, acc


def _nsa_sample_kernel(pt_ref, q_ref, gn_ref, kc_ref, vc_ref, page_ref, snew_ref, wnew_ref, win_ref, o_ref,
                       qh_scr, sel_scr, oc_scr, m_scr, l_scr, acc_scr, *, nt, past, n_pages):
    del pt_ref
    p = pl.program_id(1)
    rows = NSA_GROUP * nt
    hw = NSA_KV_HEADS * HEAD_DIM
    row1 = lax.broadcasted_iota(jnp.int32, (rows, 1), 0)
    tok = row1 % nt
    grp = row1 // nt
    pos = past + tok
    lane = lax.broadcasted_iota(jnp.int32, (rows, LANES), 1)
    blk = lax.broadcasted_iota(jnp.int32, (1, LANES), 1)
    ratio = SEL_BLOCK // CMP_STRIDE

    @pl.when(p == 0)
    def _():
        q16 = q_ref[0]
        n_cmp = kc_ref.shape[1]
        col = lax.broadcasted_iota(jnp.int32, (1, n_cmp), 1)
        cmp_idx = (col % LANES) * ratio + col // LANES
        cmp_ok = (cmp_idx * CMP_STRIDE + (CMP_BLOCK - 1)) <= pos
        cur = pos // SEL_BLOCK
        forced = (blk == 0) | (blk == cur) | (blk == cur - 1)
        future = blk * SEL_BLOCK > pos
        for h in range(NSA_KV_HEADS):
            khalf = h % 2
            ksl = slice((h // 2) * LANES, (h // 2 + 1) * LANES)
            keep = (lane // HALF) == khalf
            qh = jnp.zeros((rows, LANES), F32)
            for g in range(NSA_GROUP):
                head = h * NSA_GROUP + g
                slab = q16[:, (head // 2) * LANES:(head // 2 + 1) * LANES]
                if head % 2 != khalf:
                    slab = pltpu.roll(slab, HALF, axis=1)
                qh = jnp.where((grp == g) & keep, slab, qh)
            qh_scr[h] = qh
            sc = lax.dot_general(qh.astype(BF16), kc_ref[0, :, ksl], NT_DIMS, preferred_element_type=F32) * ATTN_SCALE
            pc = _masked_softmax_rows(sc, cmp_ok)
            oc_scr[h] = jnp.dot(pc.astype(BF16), vc_ref[0, :, ksl], preferred_element_type=F32)
            imp = pc
            for g in range(1, NSA_GROUP):
                imp = imp + pltpu.roll(pc, g * nt, axis=0)
            imp_blk = imp[:, 0:LANES]
            for j in range(1, ratio):
                imp_blk = imp_blk + imp[:, j * LANES:(j + 1) * LANES]
            score = jnp.where(future, -FORCE, jnp.where(forced, FORCE, imp_blk))
            sel_scr[h] = jnp.where(_top_blocks(score, N_SEL - 1), 1.0, 0.0)
            m_scr[h] = jnp.full((rows, 1), NEG_INF, F32)
            l_scr[h] = jnp.zeros((rows, 1), F32)
            acc_scr[h] = jnp.zeros((rows, LANES), F32)

    page = page_ref[0]
    kb = page[:, 0:hw].astype(BF16)
    vb = page[:, hw:2 * hw].astype(BF16)
    blk_row = lax.broadcasted_iota(jnp.int32, (LANES, 1), 0)
    key_lane = lax.broadcasted_iota(jnp.int32, (1, PAGE_SIZE), 1)
    expand = jnp.where(blk_row == p * (PAGE_SIZE // SEL_BLOCK) + key_lane // SEL_BLOCK, 1.0, 0.0).astype(BF16)
    for h in range(NSA_KV_HEADS):
        ksl = slice((h // 2) * LANES, (h // 2 + 1) * LANES)
        s = lax.dot_general(qh_scr[h].astype(BF16), kb[:, ksl], NT_DIMS, preferred_element_type=F32) * ATTN_SCALE
        ok = jnp.dot(sel_scr[h].astype(BF16), expand, preferred_element_type=F32) > 0.5
        x = jnp.where(ok, s, NEG_INF)
        m_i = m_scr[h]
        m_new = jnp.maximum(m_i, jnp.max(x, axis=-1, keepdims=True))
        pr = jnp.where(ok, jnp.exp(x - m_new), 0.0)
        scale = jnp.exp(m_i - m_new)
        l_scr[h] = scale * l_scr[h] + jnp.sum(pr, axis=-1, keepdims=True)
        acc_scr[h] = scale * acc_scr[h] + jnp.dot(pr.astype(BF16), vb[:, ksl], preferred_element_type=F32)
        m_scr[h] = m_new

    @pl.when(p == n_pages - 1)
    def _():
        gates = jax.nn.sigmoid(gn_ref[0])
        snew = snew_ref[0]
        wnew = wnew_ref[0]
        win = win_ref[0]
        wrows = win.shape[0]
        kwb = win[:, 0:hw].astype(BF16)
        vwb = win[:, hw:2 * hw].astype(BF16)
        widx = lax.broadcasted_iota(jnp.int32, (1, wrows), 1)
        win_ok = widx > tok + (wrows - WINDOW)
        stacks = []
        for h in range(NSA_KV_HEADS):
            ksl = slice((h // 2) * LANES, (h // 2 + 1) * LANES)
            vsl = slice(hw + (h // 2) * LANES, hw + (h // 2 + 1) * LANES)
            qb = qh_scr[h].astype(BF16)
            qf = qb.astype(F32)
            _, l_s, acc_s = _tail_keys(qf, snew[:, ksl], snew[:, vsl], tok, m_scr[h], l_scr[h], acc_scr[h])
            o_s = acc_s / l_s
            sw = lax.dot_general(qb, kwb[:, ksl], NT_DIMS, preferred_element_type=F32) * ATTN_SCALE
            x = jnp.where(win_ok, sw, NEG_INF)
            m_w = jnp.max(x, axis=-1, keepdims=True)
            pw = jnp.where(win_ok, jnp.exp(x - m_w), 0.0)
            l_w = jnp.sum(pw, axis=-1, keepdims=True)
            acc_w = jnp.dot(pw.astype(BF16), vwb[:, ksl], preferred_element_type=F32)
            _, l_w, acc_w = _tail_keys(qf, wnew[:, ksl], wnew[:, vsl], tok, m_w, l_w, acc_w)
            o_w = acc_w / l_w
            gc = jnp.zeros((rows, 1), F32)
            gs = jnp.zeros((rows, 1), F32)
            gw = jnp.zeros((rows, 1), F32)
            for g in range(NSA_GROUP):
                head = h * NSA_GROUP + g
                gc = jnp.where(grp == g, gates[:, head:head + 1], gc)
                gs = jnp.where(grp == g, gates[:, NSA_HEADS + head:NSA_HEADS + head + 1], gs)
                gw = jnp.where(grp == g, gates[:, 2 * NSA_HEADS + head:2 * NSA_HEADS + head + 1], gw)
            stacks.append(gc * oc_scr[h] + gs * o_s + gw * o_w)
        for sidx in range(NSA_HEADS // 2):
            h = sidx // 2
            halves = []
            for g in (2 * (sidx % 2), 2 * (sidx % 2) + 1):
                part = stacks[h]
                if g > 0:
                    part = pltpu.roll(part, rows - g * nt, axis=0)
                if g % 2 != h % 2:
                    part = pltpu.roll(part, HALF, axis=1)
                halves.append(part)
            slab = jnp.where(lane < HALF, halves[0], halves[1])
            o_ref[0, :, sidx * LANES:(sidx + 1) * LANES] = slab[0:nt]


def nsa_sample(q16, gn16, kvc, sel_pages, page_table_flat, snew, wnew, win_state, *, nt, past):
    bsz, rows, _ = q16.shape
    n_pages = past // PAGE_SIZE
    assert past % SEL_BLOCK == 0 and nt <= SEL_BLOCK and rows == NSA_GROUP * nt
    hw = NSA_KV_HEADS * HEAD_DIM
    per_b = lambda shape, cb=0: pl.BlockSpec((1,) + shape, lambda b, p, pt: (b, 0, cb))
    hs = NSA_KV_HEADS
    return pl.pallas_call(
        functools.partial(_nsa_sample_kernel, nt=nt, past=past, n_pages=n_pages),
        grid_spec=pltpu.PrefetchScalarGridSpec(
            num_scalar_prefetch=1,
            grid=(bsz, n_pages),
            in_specs=[
                per_b((rows, Q_COLS)), per_b((rows, GN_PAD)),
                per_b((kvc.shape[1], hw), 0), per_b((kvc.shape[1], hw), 1),
                pl.BlockSpec((1, PAGE_SIZE, KV_ROW), lambda b, p, pt: (pt[b * n_pages + p], 0, 0)),
                per_b((nt, KV_ROW)), per_b((nt, KV_ROW)),
                per_b((win_state.shape[1], KV_ROW)),
            ],
            out_specs=pl.BlockSpec((1, nt, NSA_WIDTH), lambda b, p, pt: (b, 0, 0)),
            scratch_shapes=[
                pltpu.VMEM((hs, rows, LANES), F32), pltpu.VMEM((hs, rows, LANES), F32),
                pltpu.VMEM((hs, rows, LANES), F32), pltpu.VMEM((hs, rows, 1), F32),
                pltpu.VMEM((hs, rows, 1), F32), pltpu.VMEM((hs, rows, LANES), F32),
            ],
        ),
        out_shape=jax.ShapeDtypeStruct((bsz, nt, NSA_WIDTH), F32),
        compiler_params=_cparams("parallel", "arbitrary"),
        name="nsa_sample",
    )(page_table_flat, q16, gn16, kvc, kvc, sel_pages, snew, wnew, win_state)


def _merge_kernel(za_ref, ob_ref, g0_ref, g1_ref, h_ref, wua_ref, wub_ref, wo_ref, g_ref, b_ref, o_ref, obf_ref):
    ya = jnp.dot(za_ref[...], wua_ref[...], preferred_element_type=F32)
    yb = jnp.dot(ob_ref[...], wub_ref[...], preferred_element_type=F32)
    mixed = jax.nn.sigmoid(g0_ref[...]) * ya + jax.nn.sigmoid(g1_ref[...]) * yb
    mix = jnp.dot(mixed.astype(BF16), wo_ref[...], preferred_element_type=F32)
    o = _layer_norm(DEEPNORM_ALPHA * h_ref[...] + mix, g_ref[...], b_ref[...])
    o_ref[...] = o
    obf_ref[...] = o.astype(BF16)


def merge_ln(za, ob, u, h, wua, wub, wo, g, b, *, tm):
    m, d = h.shape
    W = za.shape[1]
    assert m % tm == 0
    tile = lambda w, cb=0: pl.BlockSpec((tm, w), lambda i: (i, cb))
    const = lambda a: pl.BlockSpec(a.shape, lambda i: (0, 0), pipeline_mode=pl.Buffered(1))
    row = pl.BlockSpec((1, d), lambda i: (0, 0))
    return pl.pallas_call(
        _merge_kernel,
        grid=(m // tm,),
        in_specs=[tile(W), tile(W), tile(d, OFF_GM // d), tile(d, OFF_GM // d + 1), tile(d),
                  const(wua), const(wub), const(wo), row, row],
        out_specs=[tile(d), tile(d)],
        out_shape=[jax.ShapeDtypeStruct((m, d), F32), jax.ShapeDtypeStruct((m, d), BF16)],
        compiler_params=_cparams("parallel"),
        name="merge_ln",
    )(za, ob, u, u, h, wua, wub, wo, g.reshape(1, d), b.reshape(1, d))


TOKEN_TM = 640
PROJ_TN = 2176
MERGE_TM = 320


def kernel(x_prompt, x_sample, cache_cmp_kv, cache_sel_kv, page_table, state_win_kv, state_rwkv, state_rwkv_shift,
           ln1_g, ln1_b, ffn1_w_gu, ffn1_w_down, w_in, rwkv_mu, rwkv_w0, rwkv_w2, rwkv_a0, rwkv_a2, rwkv_g2,
           rwkv_k_k, rwkv_k_a, rwkv_r_k, rwkv_ln_w, rwkv_ln_b, w_up_a, cmp_pe, cmp_wa, cmp_wb, w_up_b, w_o,
           ln2_g, ln2_b, ffn2_w_gu, ffn2_w_down, ln3_g, ln3_b):
    bp, tp, d = x_prompt.shape
    bs, ts, _ = x_sample.shape
    assert bp == 1
    mp, ms = bp * tp, bs * ts
    n_pool = cache_cmp_kv.shape[0]
    n_pages = page_table.shape[1]
    past = n_pages * PAGE_SIZE
    kvh = (2, NSA_KV_HEADS, HEAD_DIM)
    bf = lambda a: a.astype(BF16)
    f32 = lambda a: a.astype(F32)

    x = jnp.concatenate([f32(x_prompt).reshape(mp, d), f32(x_sample).reshape(ms, d)], axis=0)
    h1, h1_bf = ffn_ln(x, bf(ffn1_w_gu), bf(ffn1_w_down), f32(ln1_g), f32(ln1_b), tm=TOKEN_TM)
    u = matmul(h1_bf, bf(pad_in_cols(w_in)), tm=TOKEN_TM, tn=PROJ_TN)
    u_s = u[mp:].reshape(bs, ts, IN_PAD)
    kv_p = u[:mp, OFF_KV:OFF_KV + KV_COLS]
    cmp_p, sel_p, win_p = kv_p[:, 0:KV_ROW], kv_p[:, KV_ROW:2 * KV_ROW], kv_p[:, 2 * KV_ROW:3 * KV_ROW]
    cmp_s = u_s[:, :, OFF_KV:OFF_KV + KV_ROW]
    sel_s = u_s[:, :, OFF_KV + KV_ROW:OFF_KV + 2 * KV_ROW]
    win_s = u_s[:, :, OFF_KV + 2 * KV_ROW:OFF_KV + 3 * KV_ROW]

    rc = rwkv_consts(rwkv_mu, rwkv_w0, rwkv_w2, rwkv_a0, rwkv_a2, rwkv_g2, rwkv_k_k, rwkv_k_a)
    za_p, s_p = rwkv_time_mix(u, 0, mp, bp, tp, None, None, rc, rwkv_r_k, rwkv_ln_w, rwkv_ln_b, prompt=True)
    za_s, s_s = rwkv_time_mix(u, mp, ms, bs, ts, state_rwkv_shift, state_rwkv, rc, rwkv_r_k, rwkv_ln_w, rwkv_ln_b,
                              prompt=False)

    cc = cmp_consts(cmp_pe, cmp_wa, cmp_wb)
    lanes_per_page = CHUNKS_PER_PAGE, CMP_STRIDE * KV_ROW
    kvc_p = compress(cmp_p.reshape(mp // PAGE_SIZE, *lanes_per_page), jnp.arange(mp // PAGE_SIZE, dtype=jnp.int32),
                     1, mp // PAGE_SIZE, cc)[0]
    o_p = nsa_prompt(u, mp, kvc_p, bf(sel_p), bf(win_p))
    pt_flat = page_table.reshape(-1).astype(jnp.int32)
    kvc_s = compress(f32(cache_cmp_kv).reshape(n_pool, *lanes_per_page), pt_flat, bs, n_pages, cc)
    q16 = jnp.tile(u_s[:, :, OFF_Q:OFF_Q + Q_COLS], (1, NSA_GROUP, 1))
    gn16 = jnp.tile(u_s[:, :, OFF_GN:OFF_GN + GN_PAD], (1, NSA_GROUP, 1))
    win_state = f32(state_win_kv).reshape(bs, -1, KV_ROW)
    o_s = nsa_sample(q16, gn16, kvc_s, f32(cache_sel_kv).reshape(n_pool, PAGE_SIZE, KV_ROW), pt_flat, sel_s, win_s,
                     win_state, nt=ts, past=past)

    za = jnp.concatenate([za_p, za_s], axis=0)
    ob = jnp.concatenate([o_p, bf(o_s.reshape(ms, NSA_WIDTH))], axis=0)
    h2, _ = merge_ln(za, ob, u, h1, bf(w_up_a), bf(w_up_b), bf(w_o), f32(ln2_g), f32(ln2_b), tm=MERGE_TM)
    y, _ = ffn_ln(h2, bf(ffn2_w_gu), bf(ffn2_w_down), f32(ln3_g), f32(ln3_b), tm=TOKEN_TM)

    wb = min(WINDOW, tp)
    wkeep = min(WINDOW, win_state.shape[1] + ts)
    win_all = jnp.concatenate([win_state, win_s], axis=1)
    cd, sd, wd = cache_cmp_kv.dtype, cache_sel_kv.dtype, state_win_kv.dtype
    rd, hd = state_rwkv.dtype, state_rwkv_shift.dtype
    return (
        y[:mp].reshape(bp, tp, d).astype(x_prompt.dtype),
        y[mp:].reshape(bs, ts, d).astype(x_sample.dtype),
        cmp_p.reshape(bp, tp, *kvh).astype(cd),
        sel_p.reshape(bp, tp, *kvh).astype(sd),
        win_p[tp - wb:].reshape(bp, wb, *kvh).astype(wd),
        s_p.astype(rd),
        u[mp - 1:mp, OFF_UA:OFF_UA + RWKV_COLS].astype(hd),
        cmp_s.reshape(bs, ts, *kvh).astype(cd),
        sel_s.reshape(bs, ts, *kvh).astype(sd),
        win_all[:, win_all.shape[1] - wkeep:].reshape(bs, wkeep, *kvh).astype(wd),
        s_s.astype(rd),
        u_s[:, ts - 1, OFF_UA:OFF_UA + RWKV_COLS].astype(hd),
    )
```

```python
import functools
import math

import jax
import jax.numpy as jnp
from jax import lax
from jax.experimental import pallas as pl
from jax.experimental.pallas import tpu as pltpu

F32 = jnp.float32
BF16 = jnp.bfloat16

D_MODEL = 2048
RWKV_WIDTH = D_MODEL // 2
RWKV_HEAD_SIZE = 64
RWKV_HEADS = RWKV_WIDTH // RWKV_HEAD_SIZE
DECAY_LORA = 64
AAA_LORA = 64
GATE_LORA = 160
RWKV_GN_EPS = 64e-5
NSA_WIDTH = D_MODEL // 2
HEAD_DIM = 64
NSA_HEADS = NSA_WIDTH // HEAD_DIM
NSA_KV_HEADS = 4
NSA_GROUP = NSA_HEADS // NSA_KV_HEADS
CMP_STRIDE = 16
CMP_BLOCK = 2 * CMP_STRIDE
CMP_RANK = 4
SEL_BLOCK = 64
N_SEL = 16
WINDOW = 512
Q_BLOCK = 128
PAGE_SIZE = 128
ATTN_SCALE = HEAD_DIM ** -0.5
FFN_HIDDEN = 256 * math.ceil(8 * D_MODEL / 3 / 256)
DEPTH = 1
DEEPNORM_ALPHA = (2 * DEPTH) ** 0.25
LN_EPS = 1e-5
NEG_INF = -1e30
FORCE = 1e9
RWKV_COLS = 3 * RWKV_WIDTH + DECAY_LORA + AAA_LORA + GATE_LORA
Q_COLS = NSA_HEADS * HEAD_DIM
KV_COLS = 3 * 2 * NSA_KV_HEADS * HEAD_DIM
NSA_GATE_COLS = 3 * NSA_HEADS
MERGE_GATE_COLS = 2 * D_MODEL

LANES = 128
SUBLANES = 8
VMEM_LIMIT_BYTES = 56 * 1024 * 1024

UA_PAD = 3584
GN_PAD = 128
OFF_GM = 0
OFF_Q = OFF_GM + MERGE_GATE_COLS
OFF_KV = OFF_Q + Q_COLS
OFF_UA = 2 * UA_PAD
OFF_GN = OFF_UA + UA_PAD
IN_PAD = OFF_GN + GN_PAD


def pad_in_cols(w):
    o = 0
    ua = w[..., o:o + RWKV_COLS]; o += RWKV_COLS
    q = w[..., o:o + Q_COLS]; o += Q_COLS
    kv = w[..., o:o + KV_COLS]; o += KV_COLS
    gn = w[..., o:o + NSA_GATE_COLS]; o += NSA_GATE_COLS
    gm = w[..., o:o + MERGE_GATE_COLS]
    z = lambda n: jnp.zeros(w.shape[:-1] + (n,), w.dtype)
    return jnp.concatenate([gm, q, kv, z(OFF_UA - (OFF_KV + KV_COLS)), ua, z(UA_PAD - RWKV_COLS), gn,
                            z(GN_PAD - NSA_GATE_COLS)], axis=-1)


def _cparams(*sem):
    return pltpu.CompilerParams(dimension_semantics=sem, vmem_limit_bytes=VMEM_LIMIT_BYTES)


def _layer_norm(y, g, b):
    mu = jnp.mean(y, axis=-1, keepdims=True)
    d = y - mu
    var = jnp.mean(d * d, axis=-1, keepdims=True)
    return d * lax.rsqrt(var + LN_EPS) * g + b


def _ffn_ln_kernel(x_ref, wg_ref, wu_ref, wd_ref, g_ref, b_ref, o_ref, ob_ref, xb_ref, acc_ref, *, n_chunks):
    j = pl.program_id(1)

    @pl.when(j == 0)
    def _():
        xb_ref[...] = x_ref[...].astype(BF16)
        acc_ref[...] = jnp.zeros_like(acc_ref)

    xb = xb_ref[...]
    gate = jnp.dot(xb, wg_ref[...], preferred_element_type=F32)
    up = jnp.dot(xb, wu_ref[...], preferred_element_type=F32)
    act = (gate * jax.nn.sigmoid(gate)) * up
    acc_ref[...] += jnp.dot(act.astype(BF16), wd_ref[...], preferred_element_type=F32)

    @pl.when(j == n_chunks - 1)
    def _():
        y = DEEPNORM_ALPHA * x_ref[...] + 0.5 * acc_ref[...]
        o = _layer_norm(y, g_ref[...], b_ref[...])
        o_ref[...] = o
        ob_ref[...] = o.astype(BF16)


def ffn_ln(x, w_gu_bf, w_down_bf, g, b, *, tm, tf=512):
    m, d = x.shape
    f = w_down_bf.shape[0]
    n_chunks = f // tf
    assert m % tm == 0 and f % tf == 0
    return pl.pallas_call(
        functools.partial(_ffn_ln_kernel, n_chunks=n_chunks),
        grid=(m // tm, n_chunks),
        in_specs=[
            pl.BlockSpec((tm, d), lambda i, j: (i, 0)),
            pl.BlockSpec((d, tf), lambda i, j: (0, j)),
            pl.BlockSpec((d, tf), lambda i, j: (0, j + n_chunks)),
            pl.BlockSpec((tf, d), lambda i, j: (j, 0)),
            pl.BlockSpec((1, d), lambda i, j: (0, 0)),
            pl.BlockSpec((1, d), lambda i, j: (0, 0)),
        ],
        out_specs=[
            pl.BlockSpec((tm, d), lambda i, j: (i, 0)),
            pl.BlockSpec((tm, d), lambda i, j: (i, 0)),
        ],
        out_shape=[jax.ShapeDtypeStruct((m, d), F32), jax.ShapeDtypeStruct((m, d), BF16)],
        scratch_shapes=[pltpu.VMEM((tm, d), BF16), pltpu.VMEM((tm, d), F32)],
        compiler_params=_cparams("parallel", "arbitrary"),
        name="ffn_ln",
    )(x, w_gu_bf, w_gu_bf, w_down_bf, g.reshape(1, d), b.reshape(1, d))


def _matmul_kernel(x_ref, w_ref, o_ref):
    o_ref[...] = jnp.dot(x_ref[...], w_ref[...], preferred_element_type=F32)


def matmul(x_bf, w_bf, *, tm, tn):
    m, k = x_bf.shape
    n = w_bf.shape[1]
    assert m % tm == 0 and n % tn == 0
    return pl.pallas_call(
        _matmul_kernel,
        grid=(m // tm, n // tn),
        in_specs=[pl.BlockSpec((tm, k), lambda i, j: (i, 0)), pl.BlockSpec((k, tn), lambda i, j: (0, j))],
        out_specs=pl.BlockSpec((tm, tn), lambda i, j: (i, j)),
        out_shape=jax.ShapeDtypeStruct((m, n), F32),
        compiler_params=_cparams("parallel", "arbitrary"),
        name="proj_matmul",
    )(x_bf, w_bf)


def _split2(x):
    hi = x.astype(BF16)
    lo = (x - hi.astype(F32)).astype(BF16)
    return hi, lo


def _split3(x):
    hi = x.astype(BF16)
    r1 = x - hi.astype(F32)
    mid = r1.astype(BF16)
    lo = (r1 - mid.astype(F32)).astype(BF16)
    return hi, mid, lo


def _head_sum(x, bo):
    outs = []
    for c in range(x.shape[1] // LANES):
        hi, mid, lo = _split3(x[:, c * LANES:(c + 1) * LANES])
        s = jnp.dot(hi, bo, preferred_element_type=F32)
        s += jnp.dot(mid, bo, preferred_element_type=F32)
        s += jnp.dot(lo, bo, preferred_element_type=F32)
        outs.append(s)
    return jnp.concatenate(outs, axis=1)


def _rwkv_prep_kernel(ua_ref, first_ref, mu_ref, w0_ref, a0_ref, kk_ref, ka_ref, w2_ref, a2_ref, g2_ref, bo_ref,
                      r_out, lw_out, kx_out, v_out, kkn_out, b_out, g_out, *, period):
    tm = ua_ref.shape[0]
    W = RWKV_WIDTH
    uf = ua_ref[...]
    rolled = pltpu.roll(uf, 1, axis=0)
    row = lax.broadcasted_iota(jnp.int32, (tm, 1), 0)
    if period >= tm:
        first = jnp.broadcast_to(first_ref[0:1, :], uf.shape)
        is_first = row == 0
    else:
        first = first_ref[...]
        is_first = (row % period) == 0
    prev = jnp.where(is_first, first, rolled)
    um = uf + (prev - uf) * mu_ref[...]
    r = um[:, 0:W]
    k = um[:, W:2 * W]
    v = um[:, 2 * W:3 * W]
    tail = um[:, 3 * W:UA_PAD]
    lw = jnp.dot(jnp.tanh(tail).astype(BF16), w2_ref[...], preferred_element_type=F32)
    z = -(w0_ref[...] + lw)
    softplus = jnp.maximum(z, 0.0) + jnp.log1p(jnp.exp(-jnp.abs(z)))
    w_log = -softplus - 0.5
    log_decay = -jnp.exp(w_log)
    a = jax.nn.sigmoid(a0_ref[...] + jnp.dot(tail.astype(BF16), a2_ref[...], preferred_element_type=F32))
    g = jnp.dot(jax.nn.sigmoid(tail).astype(BF16), g2_ref[...], preferred_element_type=F32)
    kk = k * kk_ref[...]
    n2 = _head_sum(kk * kk, bo_ref[...])
    kk = kk / jnp.maximum(jnp.sqrt(n2), 1e-12)
    kx = k * (1.0 + (a - 1.0) * ka_ref[...])
    r_out[...] = r
    lw_out[...] = log_decay
    kx_out[...] = kx
    v_out[...] = v
    kkn_out[...] = kk
    b_out[...] = kk * a
    g_out[...] = g


def rwkv_prep(u, first, consts, *, m0, m, tm, period):
    mu, w0, a0, k_k, k_a, w2p, a2p, g2p, bo = consts
    W = RWKV_WIDTH
    nt = m // tm
    b0 = m0 // tm
    assert m % tm == 0 and m0 % tm == 0
    first_rows = first.shape[0] // nt
    row_spec = lambda width: pl.BlockSpec((1, width), lambda i: (0, 0))
    full = lambda a: pl.BlockSpec(a.shape, lambda i: (0, 0))
    out_tile = pl.BlockSpec((tm, W), lambda i: (i, 0))
    return pl.pallas_call(
        functools.partial(_rwkv_prep_kernel, period=period),
        grid=(nt,),
        in_specs=[
            pl.BlockSpec((tm, UA_PAD), lambda i: (i + b0, OFF_UA // UA_PAD)),
            pl.BlockSpec((first_rows, UA_PAD), lambda i: (i, 0)),
            row_spec(UA_PAD), row_spec(W), row_spec(W), row_spec(W), row_spec(W),
            full(w2p), full(a2p), full(g2p), full(bo),
        ],
        out_specs=[out_tile] * 7,
        out_shape=[jax.ShapeDtypeStruct((m, W), F32)] * 7,
        compiler_params=_cparams("parallel"),
        name="rwkv_prep",
    )(u, first, mu, w0, a0, k_k, k_a, w2p, a2p, g2p, bo)


RWKV_CHUNK = 64
INV_BASE = 16
TN_DIMS = (((0,), (0,)), ((), ()))


def _mm3(a, b, dims=None):
    ah, al = _split2(a)
    bh, bl = _split2(b)
    if dims is None:
        dot = lambda x, y: jnp.dot(x, y, preferred_element_type=F32)
    else:
        dot = lambda x, y: lax.dot_general(x, y, dims, preferred_element_type=F32)
    return dot(ah, bh) + dot(ah, bl) + dot(al, bh)


def _mm_exact_lhs(a_bf, b):
    hi, mid, lo = _split3(b)
    out = jnp.dot(a_bf, hi, preferred_element_type=F32)
    out += jnp.dot(a_bf, mid, preferred_element_type=F32)
    out += jnp.dot(a_bf, lo, preferred_element_type=F32)
    return out


def _rwkv_chunk_kernel(r_ref, lw_ref, kx_ref, kk_ref, b_ref, v_ref, s0_ref, y_ref, sout_ref, s_scr, *, n_tb):
    tb = pl.program_id(1)
    C = RWKV_CHUNK
    R2 = 2 * C

    @pl.when(tb == 0)
    def _():
        s_scr[...] = s0_ref[0]

    ri = lax.broadcasted_iota(jnp.int32, (R2, R2), 0)
    ci = lax.broadcasted_iota(jnp.int32, (R2, R2), 1)
    same_head = (ri // C) == (ci // C)
    lower_strict = same_head & ((ri % C) > (ci % C))
    lower_incl = same_head & ((ri % C) >= (ci % C))
    eye = ri == ci
    same_base = (ri // INV_BASE) == (ci // INV_BASE)
    tri = jnp.where((lax.broadcasted_iota(jnp.int32, (C, C), 0) >= lax.broadcasted_iota(jnp.int32, (C, C), 1)),
                    1.0, 0.0).astype(BF16)
    lane = lax.broadcasted_iota(jnp.int32, (C, LANES), 1)
    left = lane < HALF
    lane_s = lax.broadcasted_iota(jnp.int32, (HALF, LANES), 1)
    left_s = lane_s < HALF

    def stack(x):
        return jnp.concatenate([jnp.where(left, x, 0.0), jnp.where(left, 0.0, x)], axis=0)

    def tile2(x):
        return jnp.concatenate([x, x], axis=0)

    slabs = range(RWKV_HEADS // 2)
    cols = [slice(p * LANES, (p + 1) * LANES) for p in slabs]
    each = lambda f, *xs: [f(*a) for a in zip(*xs)]
    r = [r_ref[0, :, cs] for cs in cols]
    lw = [lw_ref[0, :, cs] for cs in cols]
    kx = [kx_ref[0, :, cs] for cs in cols]
    kk = [kk_ref[0, :, cs] for cs in cols]
    b = [b_ref[0, :, cs] for cs in cols]
    v = [v_ref[0, :, cs] for cs in cols]
    cl = each(lambda x: _mm_exact_lhs(tri, x), lw)
    cl_end = each(lambda x: x[C - 1:C], cl)
    g_inv = each(lambda x: jnp.exp(-x), cl)
    g_end = each(lambda e, x: jnp.exp(e - x), cl_end, cl)
    kt = each(lambda k_, c_, l_: k_ * jnp.exp(c_ - l_), kk, cl, lw)
    rt = each(lambda r_, c_: r_ * jnp.exp(c_), r, cl)
    bh = each(jnp.multiply, b, g_inv)
    kh = each(jnp.multiply, kx, g_inv)
    bbar = each(jnp.multiply, b, g_end)
    kbar = each(jnp.multiply, kx, g_end)
    gram = each(lambda kt_, rt_, bh_, kh_: _mm3(jnp.concatenate([kt_, rt_], axis=0),
                                                jnp.concatenate([stack(bh_), stack(kh_)], axis=0), NT_DIMS),
                kt, rt, bh, kh)
    n_mat = each(lambda g_: jnp.where(lower_strict, tile2(g_[0:C, 0:R2]), 0.0), gram)
    ak = each(lambda g_: jnp.where(lower_strict, tile2(g_[0:C, R2:2 * R2]), 0.0), gram)
    bb = each(lambda g_: jnp.where(lower_incl, tile2(g_[C:R2, 0:R2]), 0.0), gram)
    bk = each(lambda g_: jnp.where(lower_incl, tile2(g_[C:R2, R2:2 * R2]), 0.0), gram)
    d1 = each(lambda n_: jnp.where(same_base, n_, 0.0), n_mat)
    d2 = each(_mm3, d1, d1)
    d4 = each(_mm3, d2, d2)
    d8 = each(_mm3, d4, d4)
    t_inv = each(lambda d_: jnp.where(eye, 1.0, 0.0) - d_, d1)
    for dk in (d2, d4, d8):
        t_inv = each(lambda t_, d_: t_ + _mm3(t_, d_), t_inv, dk)
    blk = INV_BASE
    while blk < C:
        off_mask = ((ri // (2 * blk)) == (ci // (2 * blk))) & ((ri // blk) != (ci // blk))
        tmp = each(lambda t_, n_: _mm3(t_, jnp.where(off_mask, n_, 0.0)), t_inv, n_mat)
        t_inv = each(lambda t_, m_: t_ - _mm3(m_, t_), t_inv, tmp)
        blk *= 2
    v_st = each(stack, v)
    kt_st = each(stack, kt)
    w_st = each(_mm3, ak, v_st)
    pq = each(lambda t_, k_, w_: _mm3(t_, jnp.concatenate([k_, w_], axis=1)), t_inv, kt_st, w_st)
    z = each(lambda bb_, bk_, pq_, v_: _mm3(
        jnp.concatenate([-bb_, bk_], axis=1),
        jnp.concatenate([pq_, jnp.concatenate([jnp.zeros_like(v_), v_], axis=1)], axis=0)), bb, bk, pq, v_st)
    ry_st = each(lambda rt_, z_: stack(rt_) + z_[:, 0:LANES], rt, z)
    bbar_st = each(stack, bbar)
    g_mat = each(lambda e_, pq_, bs_: jnp.where(eye, jnp.broadcast_to(jnp.exp(e_), (R2, LANES)), 0.0)
                 - _mm3(pq_[:, 0:LANES].T, bs_), cl_end, pq, bbar_st)
    h_bd = each(lambda v_, pq_, kb_, bs_: _mm3(jnp.concatenate([v_, pq_[:, LANES:2 * LANES]], axis=0).T,
                                               jnp.concatenate([stack(kb_), -bs_], axis=0)), v_st, pq, kbar, bbar_st)
    for p in slabs:
        s_old = s_scr[p]
        s_dup = jnp.concatenate([jnp.where(left_s, s_old, 0.0), jnp.where(left_s, 0.0, s_old)], axis=0)
        y_st = _mm3(ry_st[p], s_dup, NT_DIMS) + z[p][:, LANES:2 * LANES]
        y_ref[0, :, cols[p]] = y_st[0:C] + y_st[C:R2]
        s_scr[p] = _mm3(s_old, g_mat[p]) + (h_bd[p][0:HALF] + h_bd[p][HALF:LANES])

    @pl.when(tb == n_tb - 1)
    def _():
        sout_ref[0] = s_scr[...]


def rwkv_chunk_scan(r, lw, kx, kk, b, v, s0_slabs):
    bsz, t, W = r.shape
    n_tb = t // RWKV_CHUNK
    assert t % RWKV_CHUNK == 0
    seq = pl.BlockSpec((1, RWKV_CHUNK, W), lambda bi, ti: (bi, ti, 0))
    st = pl.BlockSpec((1, RWKV_HEADS // 2, 64, LANES), lambda bi, ti: (bi, 0, 0, 0))
    return pl.pallas_call(
        functools.partial(_rwkv_chunk_kernel, n_tb=n_tb),
        grid=(bsz, n_tb),
        in_specs=[seq] * 6 + [st],
        out_specs=[seq, st],
        out_shape=[jax.ShapeDtypeStruct((bsz, t, W), F32),
                   jax.ShapeDtypeStruct((bsz, RWKV_HEADS // 2, 64, LANES), F32)],
        scratch_shapes=[pltpu.VMEM((RWKV_HEADS // 2, 64, LANES), F32)],
        compiler_params=_cparams("parallel", "arbitrary"),
        name="rwkv_chunk_scan",
    )(r, lw, kx, kk, b, v, s0_slabs)


def _rwkv_post_kernel(y_ref, r_ref, kx_ref, v_ref, g_ref, rk_ref, lnw_ref, lnb_ref, bo_ref, o_ref):
    bo = bo_ref[...]
    y = y_ref[...]
    inv_n = 1.0 / RWKV_HEAD_SIZE
    ym = _head_sum(y, bo) * inv_n
    d = y - ym
    yv = _head_sum(d * d, bo) * inv_n
    yn = d * lax.rsqrt(yv + RWKV_GN_EPS) * lnw_ref[...] + lnb_ref[...]
    bonus = _head_sum(r_ref[...] * kx_ref[...] * rk_ref[...], bo) * v_ref[...]
    o_ref[...] = ((yn + bonus) * g_ref[...]).astype(BF16)


def rwkv_post(y, r, kx, v, g, r_k, ln_w, ln_b, bo, *, tm):
    m, W = y.shape
    tile = pl.BlockSpec((tm, W), lambda i: (i, 0))
    row = pl.BlockSpec((1, W), lambda i: (0, 0))
    return pl.pallas_call(
        _rwkv_post_kernel,
        grid=(m // tm,),
        in_specs=[tile] * 5 + [row] * 3 + [pl.BlockSpec(bo.shape, lambda i: (0, 0))],
        out_specs=tile,
        out_shape=jax.ShapeDtypeStruct((m, W), BF16),
        compiler_params=_cparams("parallel"),
        name="rwkv_post",
    )(y, r, kx, v, g, r_k, ln_w, ln_b, bo)


def rwkv_consts(mu, w0, w2, a0, a2, g2, k_k, k_a):
    W = RWKV_WIDTH
    tail = UA_PAD - 3 * W
    mu_p = jnp.zeros((1, UA_PAD), F32).at[0, :RWKV_COLS].set(mu.astype(F32))
    o_a = DECAY_LORA
    o_g = DECAY_LORA + AAA_LORA
    w2p = jnp.zeros((tail, W), BF16).at[0:o_a].set(w2.astype(BF16))
    a2p = jnp.zeros((tail, W), BF16).at[o_a:o_g].set(a2.astype(BF16))
    g2p = jnp.zeros((tail, W), BF16).at[o_g:o_g + GATE_LORA].set(g2.astype(BF16))
    half = jnp.arange(LANES) // RWKV_HEAD_SIZE
    bo = (half[:, None] == half[None, :]).astype(BF16)
    row = lambda p: p.astype(F32).reshape(1, W)
    return (mu_p, row(w0), row(a0), row(k_k), row(k_a), w2p, a2p, g2p, bo)


def _state_to_slabs(s):
    bsz = s.shape[0]
    hp = RWKV_HEADS // 2
    return s.reshape(bsz, hp, 2, 64, 64).transpose(0, 1, 3, 2, 4).reshape(bsz, hp, 64, LANES)


def _slabs_to_state(s):
    bsz = s.shape[0]
    hp = RWKV_HEADS // 2
    return s.reshape(bsz, hp, 64, 2, 64).transpose(0, 1, 3, 2, 4).reshape(bsz, RWKV_HEADS, 64, 64)


PREP_TM = 256


def rwkv_time_mix(u, m0, m, bsz, t, shift_state, s0, consts, r_k, ln_w, ln_b, *, prompt):
    W = RWKV_WIDTH
    bo = consts[-1]
    if prompt:
        tm = min(PREP_TM, m)
        nt = m // tm
        prev_rows = u[m0 + tm - 1:m0 + m - 1:tm, OFF_UA:OFF_UA + UA_PAD]
        rows = jnp.concatenate([jnp.zeros((1, UA_PAD), F32), prev_rows], axis=0)
        first = jnp.zeros((nt, SUBLANES, UA_PAD), F32).at[:, 0].set(rows).reshape(nt * SUBLANES, UA_PAD)
        period = tm
        s0_slabs = jnp.zeros((bsz, RWKV_HEADS // 2, 64, LANES), F32)
    else:
        tm = m
        sp = jnp.zeros((bsz, UA_PAD), F32).at[:, :RWKV_COLS].set(shift_state.astype(F32))
        first = jnp.repeat(sp, t, axis=0)
        period = t
        s0_slabs = _state_to_slabs(s0.astype(F32))
    r, lw, kx, v, kk, b, g = rwkv_prep(u, first, consts, m0=m0, m=m, tm=tm, period=period)
    tpad = -(-t // RWKV_CHUNK) * RWKV_CHUNK
    seq = lambda a: jnp.pad(a.reshape(bsz, t, W), ((0, 0), (0, tpad - t), (0, 0)))
    y3, s_fin = rwkv_chunk_scan(seq(r), seq(lw), seq(kx), seq(kk), seq(b), seq(v), s0_slabs)
    y = y3[:, :t].reshape(m, W)
    za = rwkv_post(y, r, kx, v, g, r_k.astype(F32).reshape(1, W), ln_w.astype(F32).reshape(1, W),
                   ln_b.astype(F32).reshape(1, W), bo, tm=tm)
    return za, _slabs_to_state(s_fin)


KV_ROW = 2 * NSA_KV_HEADS * HEAD_DIM
CHUNKS_PER_PAGE = PAGE_SIZE // CMP_STRIDE
PROJ_W = CMP_RANK * KV_ROW


PAGES_PER_STEP = 16


def _page_specs(block, n_pages):
    def spec(i):
        return pl.BlockSpec(block, lambda b, p, pt: (pt[b * n_pages + p * PAGES_PER_STEP + i], 0, 0))
    return [spec(i) for i in range(PAGES_PER_STEP)]


def _cmp_proj_kernel(pt_ref, *refs):
    del pt_ref
    x_refs = refs[:PAGES_PER_STEP]
    waf_ref, was_ref, f_ref, s_ref = refs[PAGES_PER_STEP:]
    for i, x_ref in enumerate(x_refs):
        rows = slice(i * CHUNKS_PER_PAGE, (i + 1) * CHUNKS_PER_PAGE)
        for r in range(CMP_RANK):
            f = jnp.zeros((CHUNKS_PER_PAGE, KV_ROW), F32)
            s = jnp.zeros((CHUNKS_PER_PAGE, KV_ROW), F32)
            for l in range(CMP_STRIDE):
                xl = x_ref[0, :, l * KV_ROW:(l + 1) * KV_ROW]
                f += xl * waf_ref[r * CMP_STRIDE + l:r * CMP_STRIDE + l + 1, :]
                s += xl * was_ref[r * CMP_STRIDE + l:r * CMP_STRIDE + l + 1, :]
            f_ref[0, rows, r * KV_ROW:(r + 1) * KV_ROW] = f
            s_ref[0, rows, r * KV_ROW:(r + 1) * KV_ROW] = s


def cmp_project(pages, page_table_flat, bsz, n_pages, waf, was):
    assert n_pages % PAGES_PER_STEP == 0
    n_ch = n_pages * CHUNKS_PER_PAGE
    step_ch = PAGES_PER_STEP * CHUNKS_PER_PAGE
    out = jax.ShapeDtypeStruct((bsz, n_ch, PROJ_W), F32)
    o_spec = pl.BlockSpec((1, step_ch, PROJ_W), lambda b, p, pt: (b, p, 0))
    w_spec = pl.BlockSpec(waf.shape, lambda b, p, pt: (0, 0))
    return pl.pallas_call(
        _cmp_proj_kernel,
        grid_spec=pltpu.PrefetchScalarGridSpec(
            num_scalar_prefetch=1,
            grid=(bsz, n_pages // PAGES_PER_STEP),
            in_specs=_page_specs((1, CHUNKS_PER_PAGE, CMP_STRIDE * KV_ROW), n_pages) + [w_spec, w_spec],
            out_specs=[o_spec, o_spec],
        ),
        out_shape=[out, out],
        compiler_params=_cparams("parallel", "arbitrary"),
        name="cmp_project",
    )(page_table_flat, *([pages] * PAGES_PER_STEP), waf, was)


def _cmp_mix_kernel(f_ref, s_ref, waf_ref, was_ref, pex_ref, wb_ref, o_ref):
    n_ch = f_ref.shape[1]
    acc = jnp.zeros((n_ch, KV_ROW), F32)
    for r in range(CMP_RANK):
        rs = slice(r * CMP_STRIDE, (r + 1) * CMP_STRIDE)
        pe = jnp.sum(waf_ref[rs, :] * pex_ref[0:CMP_STRIDE, :], axis=0, keepdims=True)
        pe += jnp.sum(was_ref[rs, :] * pex_ref[CMP_STRIDE:CMP_BLOCK, :], axis=0, keepdims=True)
        cs = slice(r * KV_ROW, (r + 1) * KV_ROW)
        nxt = pltpu.roll(s_ref[0, :, cs], n_ch - 1, axis=0)
        hid = jax.nn.gelu(f_ref[0, :, cs] + nxt + pe)
        acc += jnp.dot(hid.astype(BF16), wb_ref[r], preferred_element_type=F32)
    o_ref[0] = acc.astype(BF16)


def cmp_mix(f, s, waf, was, pex, wb_bd):
    bsz, n_ch, _ = f.shape
    io = pl.BlockSpec((1, n_ch, PROJ_W), lambda b: (b, 0, 0))
    full2 = lambda a: pl.BlockSpec(a.shape, lambda b: (0, 0))
    return pl.pallas_call(
        _cmp_mix_kernel,
        grid=(bsz,),
        in_specs=[io, io, full2(waf), full2(was), full2(pex), pl.BlockSpec(wb_bd.shape, lambda b: (0, 0, 0))],
        out_specs=pl.BlockSpec((1, n_ch, KV_ROW), lambda b: (b, 0, 0)),
        out_shape=jax.ShapeDtypeStruct((bsz, n_ch, KV_ROW), BF16),
        compiler_params=_cparams("parallel"),
        name="cmp_mix",
    )(f, s, waf, was, pex, wb_bd)


def cmp_consts(cmp_pe, cmp_wa, cmp_wb):
    hd = NSA_KV_HEADS * HEAD_DIM
    wa_cols = jnp.repeat(cmp_wa.astype(F32).transpose(2, 1, 0), hd, axis=2)
    waf = wa_cols[:, :CMP_STRIDE].reshape(CMP_RANK * CMP_STRIDE, KV_ROW)
    was = wa_cols[:, CMP_STRIDE:].reshape(CMP_RANK * CMP_STRIDE, KV_ROW)
    pex = jnp.broadcast_to(cmp_pe.astype(F32).transpose(1, 0, 2)[:, :, None, :],
                           (CMP_BLOCK, 2, NSA_KV_HEADS, HEAD_DIM)).reshape(CMP_BLOCK, KV_ROW)
    eye = jnp.eye(NSA_KV_HEADS, dtype=F32)
    wb = cmp_wb.astype(F32)
    bd = jnp.einsum('srde,hg,st->rshdtge', wb, eye, jnp.eye(2, dtype=F32)).reshape(CMP_RANK, KV_ROW, KV_ROW)
    return waf, was, pex, bd.astype(BF16)


def compress(pages, page_table_flat, bsz, n_pages, consts):
    waf, was, pex, wb_bd = consts
    f, s = cmp_project(pages, page_table_flat, bsz, n_pages, waf, was)
    kvc = cmp_mix(f, s, waf, was, pex, wb_bd)
    ratio = SEL_BLOCK // CMP_STRIDE
    n_ch = LANES * ratio
    kvc = jnp.pad(kvc, ((0, 0), (0, n_ch - kvc.shape[1]), (0, 0)))
    return kvc.reshape(bsz, LANES, ratio, KV_ROW).transpose(0, 2, 1, 3).reshape(bsz, n_ch, KV_ROW)


NT_DIMS = (((1,), (1,)), ((), ()))
HALF = LANES // 2
KEY_TILE = 512
WIN_SPAN = WINDOW + Q_BLOCK
TOPK_UNROLL = 8


def _masked_softmax_rows(s, ok):
    x = jnp.where(ok, s, NEG_INF)
    m = jnp.max(x, axis=-1, keepdims=True)
    e = jnp.exp(x - m)
    l = jnp.sum(e, axis=-1, keepdims=True)
    return jnp.where(ok, e / l, 0.0)


def _top_blocks(score, n_keep):
    lane = lax.broadcasted_iota(jnp.int32, score.shape, 1)

    def body(d, cnt):
        rolled = pltpu.roll(score, d, axis=1)
        beats = (rolled > score) | ((rolled == score) & (lane >= d))
        return cnt + jnp.where(beats, 1.0, 0.0)

    cnt = lax.fori_loop(1, LANES, body, jnp.zeros(score.shape, F32), unroll=TOPK_UNROLL)
    return cnt < n_keep


def _unselected_blocks(score, n_keep, st_ref):
    n_rows = score.shape[0]
    st_ref[...] = score.T
    sub = lax.broadcasted_iota(jnp.int32, (SUBLANES, LANES), 0)
    cols = []
    for c in range(n_rows // LANES):
        cs = slice(c * LANES, (c + 1) * LANES)
        tiles = [st_ref[t * SUBLANES:(t + 1) * SUBLANES, cs] for t in range(LANES // SUBLANES)]
        cnts = [jnp.zeros((SUBLANES, LANES), F32) for _ in tiles]
        for j in range(LANES):
            sj = st_ref[j:j + 1, cs]
            for t, si in enumerate(tiles):
                if t * SUBLANES > j:
                    beats = sj >= si
                elif (t + 1) * SUBLANES <= j:
                    beats = sj > si
                else:
                    beats = (sj > si) | ((sj == si) & (sub > j - t * SUBLANES))
                cnts[t] = cnts[t] + jnp.where(beats, 1.0, 0.0)
        cols.append(jnp.concatenate(cnts, axis=0))
    cnt = jnp.concatenate(cols, axis=1)
    return jnp.where(cnt.T < n_keep, 0.0, 1.0).astype(BF16)


def _softmax_groups(s, bias1, ok1=None):
    nq = bias1.shape[0]
    parts = []
    for g in range(s.shape[0] // nq):
        x = s[g * nq:(g + 1) * nq] + bias1
        e = jnp.exp(x - jnp.max(x, axis=-1, keepdims=True))
        p = e * (1.0 / jnp.sum(e, axis=-1, keepdims=True))
        parts.append(p if ok1 is None else jnp.where(ok1, p, 0.0))
    return jnp.concatenate(parts, axis=0)


def _stack_group_queries(q_f32, h, khalf):
    lane = lax.broadcasted_iota(jnp.int32, (Q_BLOCK, LANES), 1)
    keep = (lane // HALF) == khalf
    parts = []
    for g in range(NSA_GROUP):
        head = h * NSA_GROUP + g
        slab = q_f32[:, (head // 2) * LANES:(head // 2 + 1) * LANES]
        if head % 2 != khalf:
            slab = pltpu.roll(slab, HALF, axis=1)
        parts.append(jnp.where(keep, slab, 0.0).astype(BF16))
    return jnp.concatenate(parts, axis=0)


def _nsa_prompt_kernel(q_ref, gn_ref, kc_ref, vc_ref, ks_ref, vs_ref, kw_ref, vw_ref, o_ref, st_scr):
    i = pl.program_id(0)
    start = i * Q_BLOCK
    assert math.log2(ATTN_SCALE).is_integer()
    q = q_ref[...] * ATTN_SCALE
    gates = jax.nn.sigmoid(gn_ref[...])
    rows = NSA_GROUP * Q_BLOCK
    pos1 = start + lax.broadcasted_iota(jnp.int32, (Q_BLOCK, 1), 0)
    n_cmp = kc_ref.shape[0]
    col = lax.broadcasted_iota(jnp.int32, (1, n_cmp), 1)
    ratio = SEL_BLOCK // CMP_STRIDE
    cmp_idx = (col % LANES) * ratio + col // LANES
    cmp_ok = (cmp_idx * CMP_STRIDE + (CMP_BLOCK - 1)) <= pos1
    cmp_bias = jnp.where(cmp_ok, 0.0, NEG_INF)
    blk = lax.broadcasted_iota(jnp.int32, (1, LANES), 1)
    cur = pos1 // SEL_BLOCK
    forced = (blk == 0) | (blk == cur) | (blk == cur - 1)
    future = blk * SEL_BLOCK > pos1
    lane_t = lax.broadcasted_iota(jnp.int32, (1, KEY_TILE), 1)
    blk_row = lax.broadcasted_iota(jnp.int32, (LANES, 1), 0)
    lane128 = lax.broadcasted_iota(jnp.int32, (Q_BLOCK, LANES), 1)
    n_tiles = start // KEY_TILE + 1
    ws = pl.multiple_of(jnp.maximum(start - WINDOW, 0), Q_BLOCK)
    kpos_w = ws + lax.broadcasted_iota(jnp.int32, (1, WIN_SPAN), 1)
    delta = pos1 - kpos_w
    win_bias = jnp.where((delta >= 0) & (delta < WINDOW), 0.0, NEG_INF)

    heads = range(NSA_KV_HEADS)
    ksls = [slice((h // 2) * LANES, (h // 2 + 1) * LANES) for h in heads]
    qhs = [_stack_group_queries(q, h, h % 2) for h in heads]
    o_cs, scores = [], []
    for h in heads:
        sc = lax.dot_general(qhs[h], kc_ref[:, ksls[h]], NT_DIMS, preferred_element_type=F32)
        pc = _softmax_groups(sc, cmp_bias, cmp_ok)
        o_cs.append(jnp.dot(pc.astype(BF16), vc_ref[:, ksls[h]], preferred_element_type=F32))
        imp = pc[0:Q_BLOCK]
        for g in range(1, NSA_GROUP):
            imp = imp + pc[g * Q_BLOCK:(g + 1) * Q_BLOCK]
        imp_blk = imp[:, 0:LANES]
        for j in range(1, ratio):
            imp_blk = imp_blk + imp[:, j * LANES:(j + 1) * LANES]
        scores.append(jnp.where(future, -FORCE, jnp.where(forced, FORCE, imp_blk)))
    unsel_all = _unselected_blocks(jnp.concatenate(scores, axis=0), N_SEL, st_scr)
    unsels = [unsel_all[h * Q_BLOCK:(h + 1) * Q_BLOCK] for h in heads]

    def sel_tile(kt, carry, causal):
        k0 = pl.multiple_of(kt * KEY_TILE, KEY_TILE)
        expand = jnp.where(blk_row == (k0 + lane_t) // SEL_BLOCK, NEG_INF, 0.0).astype(BF16)
        ss = [lax.dot_general(qhs[h], ks_ref[pl.ds(k0, KEY_TILE), ksls[h]], NT_DIMS, preferred_element_type=F32)
              for h in heads]
        biases = [jnp.dot(unsels[h], expand, preferred_element_type=F32) for h in heads]
        if causal:
            visible = (k0 + lane_t) <= pos1
            biases = [jnp.where(visible, b_, NEG_INF) for b_ in biases]
        out = []
        for h in heads:
            m_i, l_i, acc = carry[h]
            m_parts, l_parts, p_parts, scales = [], [], [], []
            for g in range(NSA_GROUP):
                rs = slice(g * Q_BLOCK, (g + 1) * Q_BLOCK)
                x = ss[h][rs] + biases[h]
                m_new = jnp.maximum(m_i[rs], jnp.max(x, axis=-1, keepdims=True))
                p = jnp.exp(x - m_new)
                sc_g = jnp.exp(m_i[rs] - m_new)
                m_parts.append(m_new)
                l_parts.append(sc_g * l_i[rs] + jnp.sum(p, axis=-1, keepdims=True))
                p_parts.append(p.astype(BF16))
                scales.append(sc_g)
            pv = jnp.dot(jnp.concatenate(p_parts, axis=0), vs_ref[pl.ds(k0, KEY_TILE), ksls[h]],
                         preferred_element_type=F32)
            out.append((jnp.concatenate(m_parts, axis=0), jnp.concatenate(l_parts, axis=0),
                        jnp.concatenate(scales, axis=0) * acc + pv))
        return tuple(out)

    init = tuple((jnp.full((rows, 1), NEG_INF, F32), jnp.zeros((rows, 1), F32), jnp.zeros((rows, LANES), F32))
                 for _ in heads)
    carry = lax.fori_loop(0, n_tiles - 1, functools.partial(sel_tile, causal=False), init)
    carry = sel_tile(n_tiles - 1, carry, True)

    out_slabs = [None] * (NSA_HEADS // 2)
    for h in heads:
        khalf = h % 2
        ksl = ksls[h]
        o_c = o_cs[h]
        o_s = carry[h][2] * (1.0 / carry[h][1])

        sw = lax.dot_general(qhs[h], kw_ref[pl.ds(ws, WIN_SPAN), ksl], NT_DIMS, preferred_element_type=F32)
        pw = _softmax_groups(sw, win_bias)
        o_w = jnp.dot(pw.astype(BF16), vw_ref[pl.ds(ws, WIN_SPAN), ksl], preferred_element_type=F32)

        for g in range(NSA_GROUP):
            head = h * NSA_GROUP + g
            rs = slice(g * Q_BLOCK, (g + 1) * Q_BLOCK)
            gc = gates[:, head:head + 1]
            gs = gates[:, NSA_HEADS + head:NSA_HEADS + head + 1]
            gw = gates[:, 2 * NSA_HEADS + head:2 * NSA_HEADS + head + 1]
            og = gc * o_c[rs] + gs * o_s[rs] + gw * o_w[rs]
            if head % 2 != khalf:
                og = pltpu.roll(og, HALF, axis=1)
            keep = (lane128 // HALF) == (head % 2)
            prev = out_slabs[head // 2]
            out_slabs[head // 2] = jnp.where(keep, og, 0.0 if prev is None else prev)
    for sidx in range(NSA_HEADS // 2):
        o_ref[:, sidx * LANES:(sidx + 1) * LANES] = out_slabs[sidx].astype(BF16)


def nsa_prompt(u, t, kvc, sel_kv_bf, win_kv_bf):
    n_qb = t // Q_BLOCK
    hw = NSA_KV_HEADS * HEAD_DIM
    whole = lambda a, cb: pl.BlockSpec((a.shape[0], hw), lambda i: (0, cb))
    return pl.pallas_call(
        _nsa_prompt_kernel,
        grid=(n_qb,),
        in_specs=[
            pl.BlockSpec((Q_BLOCK, Q_COLS), lambda i: (i, OFF_Q // Q_COLS)),
            pl.BlockSpec((Q_BLOCK, GN_PAD), lambda i: (i, OFF_GN // GN_PAD)),
            whole(kvc, 0), whole(kvc, 1),
            whole(sel_kv_bf, 0), whole(sel_kv_bf, 1),
            whole(win_kv_bf, 0), whole(win_kv_bf, 1),
        ],
        out_specs=pl.BlockSpec((Q_BLOCK, NSA_WIDTH), lambda i: (i, 0)),
        out_shape=jax.ShapeDtypeStruct((t, NSA_WIDTH), BF16),
        scratch_shapes=[pltpu.VMEM((LANES, NSA_KV_HEADS * Q_BLOCK), F32)],
        compiler_params=_cparams("arbitrary"),
        name="nsa_prompt",
    )(u, u, kvc, kvc, sel_kv_bf, sel_kv_bf, win_kv_bf, win_kv_bf)


def _tail_keys(qf, k_new, v_new, tok, m, l, acc):
    nt = k_new.shape[0]
    kn = k_new.astype(BF16).astype(F32)
    vn = v_new.astype(BF16).astype(F32)
    s = [jnp.sum(qf * kn[j:j + 1], axis=1, keepdims=True) * ATTN_SCALE for j in range(nt)]
    ok = [tok >= j for j in range(nt)]
    m_new = m
    for j in range(nt):
        m_new = jnp.maximum(m_new, jnp.where(ok[j], s[j], NEG_INF))
    scale = jnp.exp(m - m_new)
    l = scale * l
    acc = scale * acc
    for j in range(nt):
        p = jnp.where(ok[j], jnp.exp(s[j] - m_new), 0.0)
        l = l + p
        acc = acc + p * vn[j:j + 1]
    return m_new, l, acc


def _nsa_sample_kernel(pt_ref, q_ref, gn_ref, kc_ref, vc_ref, *refs, nt, past, n_steps):
    del pt_ref
    page_refs = refs[:PAGES_PER_STEP]
    (snew_ref, wnew_ref, win_ref, o_ref,
     qh_scr, sel_scr, oc_scr, m_scr, l_scr, acc_scr, kb_scr, vb_scr) = refs[PAGES_PER_STEP:]
    p = pl.program_id(1)
    rows = NSA_GROUP * nt
    hw = NSA_KV_HEADS * HEAD_DIM
    row1 = lax.broadcasted_iota(jnp.int32, (rows, 1), 0)
    tok = row1 % nt
    grp = row1 // nt
    pos = past + tok
    lane = lax.broadcasted_iota(jnp.int32, (rows, LANES), 1)
    blk = lax.broadcasted_iota(jnp.int32, (1, LANES), 1)
    ratio = SEL_BLOCK // CMP_STRIDE

    @pl.when(p == 0)
    def _():
        q16 = q_ref[0]
        n_cmp = kc_ref.shape[1]
        col = lax.broadcasted_iota(jnp.int32, (1, n_cmp), 1)
        cmp_idx = (col % LANES) * ratio + col // LANES
        cmp_ok = (cmp_idx * CMP_STRIDE + (CMP_BLOCK - 1)) <= pos
        cur = pos // SEL_BLOCK
        forced = (blk == 0) | (blk == cur) | (blk == cur - 1)
        future = blk * SEL_BLOCK > pos
        for h in range(NSA_KV_HEADS):
            khalf = h % 2
            ksl = slice((h // 2) * LANES, (h // 2 + 1) * LANES)
            keep = (lane // HALF) == khalf
            qh = jnp.zeros((rows, LANES), F32)
            for g in range(NSA_GROUP):
                head = h * NSA_GROUP + g
                slab = q16[:, (head // 2) * LANES:(head // 2 + 1) * LANES]
                if head % 2 != khalf:
                    slab = pltpu.roll(slab, HALF, axis=1)
                qh = jnp.where((grp == g) & keep, slab, qh)
            qh_scr[h] = qh
            sc = lax.dot_general(qh.astype(BF16), kc_ref[0, :, ksl], NT_DIMS, preferred_element_type=F32) * ATTN_SCALE
            pc = _masked_softmax_rows(sc, cmp_ok)
            oc_scr[h] = jnp.dot(pc.astype(BF16), vc_ref[0, :, ksl], preferred_element_type=F32)
            imp = pc
            for g in range(1, NSA_GROUP):
                imp = imp + pltpu.roll(pc, g * nt, axis=0)
            imp_blk = imp[:, 0:LANES]
            for j in range(1, ratio):
                imp_blk = imp_blk + imp[:, j * LANES:(j + 1) * LANES]
            score = jnp.where(future, -FORCE, jnp.where(forced, FORCE, imp_blk))
            sel_scr[h] = jnp.where(_top_blocks(score, N_SEL - 1), 1.0, 0.0)
            m_scr[h] = jnp.full((rows, 1), NEG_INF, F32)
            l_scr[h] = jnp.zeros((rows, 1), F32)
            acc_scr[h] = jnp.zeros((rows, LANES), F32)

    step_keys = PAGES_PER_STEP * PAGE_SIZE
    for i, page_ref in enumerate(page_refs):
        ks = slice(i * PAGE_SIZE, (i + 1) * PAGE_SIZE)
        kb_scr[ks, :] = page_ref[0, :, 0:hw].astype(BF16)
        vb_scr[ks, :] = page_ref[0, :, hw:2 * hw].astype(BF16)
    blk_row = lax.broadcasted_iota(jnp.int32, (LANES, 1), 0)
    key_lane = lax.broadcasted_iota(jnp.int32, (1, step_keys), 1)
    expand = jnp.where(blk_row == (p * step_keys + key_lane) // SEL_BLOCK, 1.0, 0.0).astype(BF16)
    for h in range(NSA_KV_HEADS):
        ksl = slice((h // 2) * LANES, (h // 2 + 1) * LANES)
        s = lax.dot_general(qh_scr[h].astype(BF16), kb_scr[:, ksl], NT_DIMS, preferred_element_type=F32) * ATTN_SCALE
        ok = jnp.dot(sel_scr[h].astype(BF16), expand, preferred_element_type=F32) > 0.5
        x = jnp.where(ok, s, NEG_INF)
        m_i = m_scr[h]
        m_new = jnp.maximum(m_i, jnp.max(x, axis=-1, keepdims=True))
        pr = jnp.where(ok, jnp.exp(x - m_new), 0.0)
        scale = jnp.exp(m_i - m_new)
        l_scr[h] = scale * l_scr[h] + jnp.sum(pr, axis=-1, keepdims=True)
        acc_scr[h] = scale * acc_scr[h] + jnp.dot(pr.astype(BF16), vb_scr[:, ksl], preferred_element_type=F32)
        m_scr[h] = m_new

    @pl.when(p == n_steps - 1)
    def _():
        gates = jax.nn.sigmoid(gn_ref[0])
        snew = snew_ref[0]
        wnew = wnew_ref[0]
        win = win_ref[0]
        wrows = win.shape[0]
        kwb = win[:, 0:hw].astype(BF16)
        vwb = win[:, hw:2 * hw].astype(BF16)
        widx = lax.broadcasted_iota(jnp.int32, (1, wrows), 1)
        win_ok = widx > tok + (wrows - WINDOW)
        stacks = []
        for h in range(NSA_KV_HEADS):
            ksl = slice((h // 2) * LANES, (h // 2 + 1) * LANES)
            vsl = slice(hw + (h // 2) * LANES, hw + (h // 2 + 1) * LANES)
            qb = qh_scr[h].astype(BF16)
            qf = qb.astype(F32)
            _, l_s, acc_s = _tail_keys(qf, snew[:, ksl], snew[:, vsl], tok, m_scr[h], l_scr[h], acc_scr[h])
            o_s = acc_s / l_s
            sw = lax.dot_general(qb, kwb[:, ksl], NT_DIMS, preferred_element_type=F32) * ATTN_SCALE
            x = jnp.where(win_ok, sw, NEG_INF)
            m_w = jnp.max(x, axis=-1, keepdims=True)
            pw = jnp.where(win_ok, jnp.exp(x - m_w), 0.0)
            l_w = jnp.sum(pw, axis=-1, keepdims=True)
            acc_w = jnp.dot(pw.astype(BF16), vwb[:, ksl], preferred_element_type=F32)
            _, l_w, acc_w = _tail_keys(qf, wnew[:, ksl], wnew[:, vsl], tok, m_w, l_w, acc_w)
            o_w = acc_w / l_w
            gc = jnp.zeros((rows, 1), F32)
            gs = jnp.zeros((rows, 1), F32)
            gw = jnp.zeros((rows, 1), F32)
            for g in range(NSA_GROUP):
                head = h * NSA_GROUP + g
                gc = jnp.where(grp == g, gates[:, head:head + 1], gc)
                gs = jnp.where(grp == g, gates[:, NSA_HEADS + head:NSA_HEADS + head + 1], gs)
                gw = jnp.where(grp == g, gates[:, 2 * NSA_HEADS + head:2 * NSA_HEADS + head + 1], gw)
            stacks.append(gc * oc_scr[h] + gs * o_s + gw * o_w)
        for sidx in range(NSA_HEADS // 2):
            h = sidx // 2
            halves = []
            for g in (2 * (sidx % 2), 2 * (sidx % 2) + 1):
                part = stacks[h]
                if g > 0:
                    part = pltpu.roll(part, rows - g * nt, axis=0)
                if g % 2 != h % 2:
                    part = pltpu.roll(part, HALF, axis=1)
                halves.append(part)
            slab = jnp.where(lane < HALF, halves[0], halves[1])
            o_ref[0, :, sidx * LANES:(sidx + 1) * LANES] = slab[0:nt]


def nsa_sample(q16, gn16, kvc, sel_pages, page_table_flat, snew, wnew, win_state, *, nt, past):
    bsz, rows, _ = q16.shape
    n_pages = past // PAGE_SIZE
    assert past % SEL_BLOCK == 0 and nt <= SEL_BLOCK and rows == NSA_GROUP * nt
    assert n_pages % PAGES_PER_STEP == 0
    n_steps = n_pages // PAGES_PER_STEP
    step_keys = PAGES_PER_STEP * PAGE_SIZE
    hw = NSA_KV_HEADS * HEAD_DIM
    per_b = lambda shape, cb=0: pl.BlockSpec((1,) + shape, lambda b, p, pt: (b, 0, cb))
    hs = NSA_KV_HEADS
    return pl.pallas_call(
        functools.partial(_nsa_sample_kernel, nt=nt, past=past, n_steps=n_steps),
        grid_spec=pltpu.PrefetchScalarGridSpec(
            num_scalar_prefetch=1,
            grid=(bsz, n_steps),
            in_specs=[
                per_b((rows, Q_COLS)), per_b((rows, GN_PAD)),
                per_b((kvc.shape[1], hw), 0), per_b((kvc.shape[1], hw), 1),
                *_page_specs((1, PAGE_SIZE, KV_ROW), n_pages),
                per_b((nt, KV_ROW)), per_b((nt, KV_ROW)),
                per_b((win_state.shape[1], KV_ROW)),
            ],
            out_specs=pl.BlockSpec((1, nt, NSA_WIDTH), lambda b, p, pt: (b, 0, 0)),
            scratch_shapes=[
                pltpu.VMEM((hs, rows, LANES), F32), pltpu.VMEM((hs, rows, LANES), F32),
                pltpu.VMEM((hs, rows, LANES), F32), pltpu.VMEM((hs, rows, 1), F32),
                pltpu.VMEM((hs, rows, 1), F32), pltpu.VMEM((hs, rows, LANES), F32),
                pltpu.VMEM((step_keys, hw), BF16), pltpu.VMEM((step_keys, hw), BF16),
            ],
        ),
        out_shape=jax.ShapeDtypeStruct((bsz, nt, NSA_WIDTH), F32),
        compiler_params=_cparams("parallel", "arbitrary"),
        name="nsa_sample",
    )(page_table_flat, q16, gn16, kvc, kvc, *([sel_pages] * PAGES_PER_STEP), snew, wnew, win_state)


def _merge_kernel(za_ref, ob_ref, g0_ref, g1_ref, h_ref, wua_ref, wub_ref, wo_ref, g_ref, b_ref, o_ref, obf_ref):
    ya = jnp.dot(za_ref[...], wua_ref[...], preferred_element_type=F32)
    yb = jnp.dot(ob_ref[...], wub_ref[...], preferred_element_type=F32)
    mixed = jax.nn.sigmoid(g0_ref[...]) * ya + jax.nn.sigmoid(g1_ref[...]) * yb
    mix = jnp.dot(mixed.astype(BF16), wo_ref[...], preferred_element_type=F32)
    o = _layer_norm(DEEPNORM_ALPHA * h_ref[...] + mix, g_ref[...], b_ref[...])
    o_ref[...] = o
    obf_ref[...] = o.astype(BF16)


def merge_ln(za, ob, u, h, wua, wub, wo, g, b, *, tm):
    m, d = h.shape
    W = za.shape[1]
    assert m % tm == 0
    tile = lambda w, cb=0: pl.BlockSpec((tm, w), lambda i: (i, cb))
    const = lambda a: pl.BlockSpec(a.shape, lambda i: (0, 0), pipeline_mode=pl.Buffered(1))
    row = pl.BlockSpec((1, d), lambda i: (0, 0))
    return pl.pallas_call(
        _merge_kernel,
        grid=(m // tm,),
        in_specs=[tile(W), tile(W), tile(d, OFF_GM // d), tile(d, OFF_GM // d + 1), tile(d),
                  const(wua), const(wub), const(wo), row, row],
        out_specs=[tile(d), tile(d)],
        out_shape=[jax.ShapeDtypeStruct((m, d), F32), jax.ShapeDtypeStruct((m, d), BF16)],
        compiler_params=_cparams("parallel"),
        name="merge_ln",
    )(za, ob, u, u, h, wua, wub, wo, g.reshape(1, d), b.reshape(1, d))


TOKEN_TM = 640
PROJ_TN = 2176
MERGE_TM = 320


def kernel(x_prompt, x_sample, cache_cmp_kv, cache_sel_kv, page_table, state_win_kv, state_rwkv, state_rwkv_shift,
           ln1_g, ln1_b, ffn1_w_gu, ffn1_w_down, w_in, rwkv_mu, rwkv_w0, rwkv_w2, rwkv_a0, rwkv_a2, rwkv_g2,
           rwkv_k_k, rwkv_k_a, rwkv_r_k, rwkv_ln_w, rwkv_ln_b, w_up_a, cmp_pe, cmp_wa, cmp_wb, w_up_b, w_o,
           ln2_g, ln2_b, ffn2_w_gu, ffn2_w_down, ln3_g, ln3_b):
    bp, tp, d = x_prompt.shape
    bs, ts, _ = x_sample.shape
    assert bp == 1
    mp, ms = bp * tp, bs * ts
    n_pool = cache_cmp_kv.shape[0]
    n_pages = page_table.shape[1]
    past = n_pages * PAGE_SIZE
    kvh = (2, NSA_KV_HEADS, HEAD_DIM)
    bf = lambda a: a.astype(BF16)
    f32 = lambda a: a.astype(F32)

    x = jnp.concatenate([f32(x_prompt).reshape(mp, d), f32(x_sample).reshape(ms, d)], axis=0)
    h1, h1_bf = ffn_ln(x, bf(ffn1_w_gu), bf(ffn1_w_down), f32(ln1_g), f32(ln1_b), tm=TOKEN_TM)
    u = matmul(h1_bf, bf(pad_in_cols(w_in)), tm=TOKEN_TM, tn=PROJ_TN)
    u_s = u[mp:].reshape(bs, ts, IN_PAD)
    kv_p = u[:mp, OFF_KV:OFF_KV + KV_COLS]
    cmp_p, sel_p, win_p = kv_p[:, 0:KV_ROW], kv_p[:, KV_ROW:2 * KV_ROW], kv_p[:, 2 * KV_ROW:3 * KV_ROW]
    cmp_s = u_s[:, :, OFF_KV:OFF_KV + KV_ROW]
    sel_s = u_s[:, :, OFF_KV + KV_ROW:OFF_KV + 2 * KV_ROW]
    win_s = u_s[:, :, OFF_KV + 2 * KV_ROW:OFF_KV + 3 * KV_ROW]

    rc = rwkv_consts(rwkv_mu, rwkv_w0, rwkv_w2, rwkv_a0, rwkv_a2, rwkv_g2, rwkv_k_k, rwkv_k_a)
    za_p, s_p = rwkv_time_mix(u, 0, mp, bp, tp, None, None, rc, rwkv_r_k, rwkv_ln_w, rwkv_ln_b, prompt=True)
    za_s, s_s = rwkv_time_mix(u, mp, ms, bs, ts, state_rwkv_shift, state_rwkv, rc, rwkv_r_k, rwkv_ln_w, rwkv_ln_b,
                              prompt=False)

    cc = cmp_consts(cmp_pe, cmp_wa, cmp_wb)
    lanes_per_page = CHUNKS_PER_PAGE, CMP_STRIDE * KV_ROW
    kvc_p = compress(cmp_p.reshape(mp // PAGE_SIZE, *lanes_per_page), jnp.arange(mp // PAGE_SIZE, dtype=jnp.int32),
                     1, mp // PAGE_SIZE, cc)[0]
    o_p = nsa_prompt(u, mp, kvc_p, bf(sel_p), bf(win_p))
    pt_flat = page_table.reshape(-1).astype(jnp.int32)
    kvc_s = compress(f32(cache_cmp_kv).reshape(n_pool, *lanes_per_page), pt_flat, bs, n_pages, cc)
    q16 = jnp.tile(u_s[:, :, OFF_Q:OFF_Q + Q_COLS], (1, NSA_GROUP, 1))
    gn16 = jnp.tile(u_s[:, :, OFF_GN:OFF_GN + GN_PAD], (1, NSA_GROUP, 1))
    win_state = f32(state_win_kv).reshape(bs, -1, KV_ROW)
    o_s = nsa_sample(q16, gn16, kvc_s, f32(cache_sel_kv).reshape(n_pool, PAGE_SIZE, KV_ROW), pt_flat, sel_s, win_s,
                     win_state, nt=ts, past=past)

    za = jnp.concatenate([za_p, za_s], axis=0)
    ob = jnp.concatenate([o_p, bf(o_s.reshape(ms, NSA_WIDTH))], axis=0)
    h2, _ = merge_ln(za, ob, u, h1, bf(w_up_a), bf(w_up_b), bf(w_o), f32(ln2_g), f32(ln2_b), tm=MERGE_TM)
    y, _ = ffn_ln(h2, bf(ffn2_w_gu), bf(ffn2_w_down), f32(ln3_g), f32(ln3_b), tm=TOKEN_TM)

    wb = min(WINDOW, tp)
    wkeep = min(WINDOW, win_state.shape[1] + ts)
    win_all = jnp.concatenate([win_state, win_s], axis=1)
    cd, sd, wd = cache_cmp_kv.dtype, cache_sel_kv.dtype, state_win_kv.dtype
    rd, hd = state_rwkv.dtype, state_rwkv_shift.dtype
    return (
        y[:mp].reshape(bp, tp, d).astype(x_prompt.dtype),
        y[mp:].reshape(bs, ts, d).astype(x_sample.dtype),
        cmp_p.reshape(bp, tp, *kvh).astype(cd),
        sel_p.reshape(bp, tp, *kvh).astype(sd),
        win_p[tp - wb:].reshape(bp, wb, *kvh).astype(wd),
        s_p.astype(rd),
        u[mp - 1:mp, OFF_UA:OFF_UA + RWKV_COLS].astype(hd),
        cmp_s.reshape(bs, ts, *kvh).astype(cd),
        sel_s.reshape(bs, ts, *kvh).astype(sd),
        win_all[:, win_all.shape[1] - wkeep:].reshape(bs, wkeep, *kvh).astype(wd),
        s_s.astype(rd),
        u_s[:, ts - 1, OFF_UA:OFF_UA + RWKV_COLS].astype(hd),
    )
```

```python
import functools
import math

import jax
import jax.numpy as jnp
from jax import lax
from jax.experimental import pallas as pl
from jax.experimental.pallas import tpu as pltpu

F32 = jnp.float32
BF16 = jnp.bfloat16

D_MODEL = 2048
RWKV_WIDTH = D_MODEL // 2
RWKV_HEAD_SIZE = 64
RWKV_HEADS = RWKV_WIDTH // RWKV_HEAD_SIZE
DECAY_LORA = 64
AAA_LORA = 64
GATE_LORA = 160
RWKV_GN_EPS = 64e-5
NSA_WIDTH = D_MODEL // 2
HEAD_DIM = 64
NSA_HEADS = NSA_WIDTH // HEAD_DIM
NSA_KV_HEADS = 4
NSA_GROUP = NSA_HEADS // NSA_KV_HEADS
CMP_STRIDE = 16
CMP_BLOCK = 2 * CMP_STRIDE
CMP_RANK = 4
SEL_BLOCK = 64
N_SEL = 16
WINDOW = 512
Q_BLOCK = 128
PAGE_SIZE = 128
ATTN_SCALE = HEAD_DIM ** -0.5
FFN_HIDDEN = 256 * math.ceil(8 * D_MODEL / 3 / 256)
DEPTH = 1
DEEPNORM_ALPHA = (2 * DEPTH) ** 0.25
LN_EPS = 1e-5
NEG_INF = -1e30
FORCE = 1e9
RWKV_COLS = 3 * RWKV_WIDTH + DECAY_LORA + AAA_LORA + GATE_LORA
Q_COLS = NSA_HEADS * HEAD_DIM
KV_COLS = 3 * 2 * NSA_KV_HEADS * HEAD_DIM
NSA_GATE_COLS = 3 * NSA_HEADS
MERGE_GATE_COLS = 2 * D_MODEL

LANES = 128
SUBLANES = 8
VMEM_LIMIT_BYTES = 56 * 1024 * 1024

UA_PAD = 3584
GN_PAD = 128
OFF_GM = 0
OFF_Q = OFF_GM + MERGE_GATE_COLS
OFF_KV = OFF_Q + Q_COLS
OFF_UA = 2 * UA_PAD
OFF_GN = OFF_UA + UA_PAD
IN_PAD = OFF_GN + GN_PAD


def pad_in_cols(w):
    o = 0
    ua = w[..., o:o + RWKV_COLS]; o += RWKV_COLS
    q = w[..., o:o + Q_COLS]; o += Q_COLS
    kv = w[..., o:o + KV_COLS]; o += KV_COLS
    gn = w[..., o:o + NSA_GATE_COLS]; o += NSA_GATE_COLS
    gm = w[..., o:o + MERGE_GATE_COLS]
    z = lambda n: jnp.zeros(w.shape[:-1] + (n,), w.dtype)
    return jnp.concatenate([gm, q, kv, z(OFF_UA - (OFF_KV + KV_COLS)), ua, z(UA_PAD - RWKV_COLS), gn,
                            z(GN_PAD - NSA_GATE_COLS)], axis=-1)


def _cparams(*sem):
    return pltpu.CompilerParams(dimension_semantics=sem, vmem_limit_bytes=VMEM_LIMIT_BYTES)


def _layer_norm(y, g, b):
    mu = jnp.mean(y, axis=-1, keepdims=True)
    d = y - mu
    var = jnp.mean(d * d, axis=-1, keepdims=True)
    return d * lax.rsqrt(var + LN_EPS) * g + b


def _ffn_ln_kernel(x_ref, wg_ref, wu_ref, wd_ref, g_ref, b_ref, o_ref, ob_ref, xb_ref, acc_ref, *, n_chunks):
    j = pl.program_id(1)

    @pl.when(j == 0)
    def _():
        xb_ref[...] = x_ref[...].astype(BF16)
        acc_ref[...] = jnp.zeros_like(acc_ref)

    xb = xb_ref[...]
    gate = jnp.dot(xb, wg_ref[...], preferred_element_type=F32)
    up = jnp.dot(xb, wu_ref[...], preferred_element_type=F32)
    act = (gate * jax.nn.sigmoid(gate)) * up
    acc_ref[...] += jnp.dot(act.astype(BF16), wd_ref[...], preferred_element_type=F32)

    @pl.when(j == n_chunks - 1)
    def _():
        y = DEEPNORM_ALPHA * x_ref[...] + 0.5 * acc_ref[...]
        o = _layer_norm(y, g_ref[...], b_ref[...])
        o_ref[...] = o
        ob_ref[...] = o.astype(BF16)


def ffn_ln(x, w_gu_bf, w_down_bf, g, b, *, tm, tf=512):
    m, d = x.shape
    f = w_down_bf.shape[0]
    n_chunks = f // tf
    assert m % tm == 0 and f % tf == 0
    return pl.pallas_call(
        functools.partial(_ffn_ln_kernel, n_chunks=n_chunks),
        grid=(m // tm, n_chunks),
        in_specs=[
            pl.BlockSpec((tm, d), lambda i, j: (i, 0)),
            pl.BlockSpec((d, tf), lambda i, j: (0, j)),
            pl.BlockSpec((d, tf), lambda i, j: (0, j + n_chunks)),
            pl.BlockSpec((tf, d), lambda i, j: (j, 0)),
            pl.BlockSpec((1, d), lambda i, j: (0, 0)),
            pl.BlockSpec((1, d), lambda i, j: (0, 0)),
        ],
        out_specs=[
            pl.BlockSpec((tm, d), lambda i, j: (i, 0)),
            pl.BlockSpec((tm, d), lambda i, j: (i, 0)),
        ],
        out_shape=[jax.ShapeDtypeStruct((m, d), F32), jax.ShapeDtypeStruct((m, d), BF16)],
        scratch_shapes=[pltpu.VMEM((tm, d), BF16), pltpu.VMEM((tm, d), F32)],
        compiler_params=_cparams("parallel", "arbitrary"),
        name="ffn_ln",
    )(x, w_gu_bf, w_gu_bf, w_down_bf, g.reshape(1, d), b.reshape(1, d))


def _matmul_kernel(x_ref, w_ref, o_ref):
    o_ref[...] = jnp.dot(x_ref[...], w_ref[...], preferred_element_type=F32)


def matmul(x_bf, w_bf, *, tm, tn):
    m, k = x_bf.shape
    n = w_bf.shape[1]
    assert m % tm == 0 and n % tn == 0
    return pl.pallas_call(
        _matmul_kernel,
        grid=(m // tm, n // tn),
        in_specs=[pl.BlockSpec((tm, k), lambda i, j: (i, 0)), pl.BlockSpec((k, tn), lambda i, j: (0, j))],
        out_specs=pl.BlockSpec((tm, tn), lambda i, j: (i, j)),
        out_shape=jax.ShapeDtypeStruct((m, n), F32),
        compiler_params=_cparams("parallel", "arbitrary"),
        name="proj_matmul",
    )(x_bf, w_bf)


def _split2(x):
    hi = x.astype(BF16)
    lo = (x - hi.astype(F32)).astype(BF16)
    return hi, lo


def _split3(x):
    hi = x.astype(BF16)
    r1 = x - hi.astype(F32)
    mid = r1.astype(BF16)
    lo = (r1 - mid.astype(F32)).astype(BF16)
    return hi, mid, lo


def _head_sum(x, bo):
    outs = []
    for c in range(x.shape[1] // LANES):
        hi, mid, lo = _split3(x[:, c * LANES:(c + 1) * LANES])
        s = jnp.dot(hi, bo, preferred_element_type=F32)
        s += jnp.dot(mid, bo, preferred_element_type=F32)
        s += jnp.dot(lo, bo, preferred_element_type=F32)
        outs.append(s)
    return jnp.concatenate(outs, axis=1)


def _rwkv_prep_kernel(ua_ref, first_ref, mu_ref, w0_ref, a0_ref, kk_ref, ka_ref, w2_ref, a2_ref, g2_ref, bo_ref,
                      r_out, lw_out, kx_out, v_out, kkn_out, b_out, g_out, *, period):
    tm = ua_ref.shape[0]
    W = RWKV_WIDTH
    uf = ua_ref[...]
    rolled = pltpu.roll(uf, 1, axis=0)
    row = lax.broadcasted_iota(jnp.int32, (tm, 1), 0)
    if period >= tm:
        first = jnp.broadcast_to(first_ref[0:1, :], uf.shape)
        is_first = row == 0
    else:
        first = first_ref[...]
        is_first = (row % period) == 0
    prev = jnp.where(is_first, first, rolled)
    um = uf + (prev - uf) * mu_ref[...]
    r = um[:, 0:W]
    k = um[:, W:2 * W]
    v = um[:, 2 * W:3 * W]
    tail = um[:, 3 * W:UA_PAD]
    lw = jnp.dot(jnp.tanh(tail).astype(BF16), w2_ref[...], preferred_element_type=F32)
    z = -(w0_ref[...] + lw)
    softplus = jnp.maximum(z, 0.0) + jnp.log1p(jnp.exp(-jnp.abs(z)))
    w_log = -softplus - 0.5
    log_decay = -jnp.exp(w_log)
    a = jax.nn.sigmoid(a0_ref[...] + jnp.dot(tail.astype(BF16), a2_ref[...], preferred_element_type=F32))
    g = jnp.dot(jax.nn.sigmoid(tail).astype(BF16), g2_ref[...], preferred_element_type=F32)
    kk = k * kk_ref[...]
    n2 = _head_sum(kk * kk, bo_ref[...])
    kk = kk / jnp.maximum(jnp.sqrt(n2), 1e-12)
    kx = k * (1.0 + (a - 1.0) * ka_ref[...])
    r_out[...] = r
    lw_out[...] = log_decay
    kx_out[...] = kx
    v_out[...] = v
    kkn_out[...] = kk
    b_out[...] = kk * a
    g_out[...] = g


def rwkv_prep(u, first, consts, *, m0, m, tm, period):
    mu, w0, a0, k_k, k_a, w2p, a2p, g2p, bo = consts
    W = RWKV_WIDTH
    nt = m // tm
    b0 = m0 // tm
    assert m % tm == 0 and m0 % tm == 0
    first_rows = first.shape[0] // nt
    row_spec = lambda width: pl.BlockSpec((1, width), lambda i: (0, 0))
    full = lambda a: pl.BlockSpec(a.shape, lambda i: (0, 0))
    out_tile = pl.BlockSpec((tm, W), lambda i: (i, 0))
    return pl.pallas_call(
        functools.partial(_rwkv_prep_kernel, period=period),
        grid=(nt,),
        in_specs=[
            pl.BlockSpec((tm, UA_PAD), lambda i: (i + b0, OFF_UA // UA_PAD)),
            pl.BlockSpec((first_rows, UA_PAD), lambda i: (i, 0)),
            row_spec(UA_PAD), row_spec(W), row_spec(W), row_spec(W), row_spec(W),
            full(w2p), full(a2p), full(g2p), full(bo),
        ],
        out_specs=[out_tile] * 7,
        out_shape=[jax.ShapeDtypeStruct((m, W), F32)] * 7,
        compiler_params=_cparams("parallel"),
        name="rwkv_prep",
    )(u, first, mu, w0, a0, k_k, k_a, w2p, a2p, g2p, bo)


RWKV_CHUNK = 64
INV_BASE = 16
TN_DIMS = (((0,), (0,)), ((), ()))


def _mm3(a, b, dims=None):
    ah, al = _split2(a)
    bh, bl = _split2(b)
    if dims is None:
        dot = lambda x, y: jnp.dot(x, y, preferred_element_type=F32)
    else:
        dot = lambda x, y: lax.dot_general(x, y, dims, preferred_element_type=F32)
    return dot(ah, bh) + dot(ah, bl) + dot(al, bh)


def _mm_exact_lhs(a_bf, b):
    hi, mid, lo = _split3(b)
    out = jnp.dot(a_bf, hi, preferred_element_type=F32)
    out += jnp.dot(a_bf, mid, preferred_element_type=F32)
    out += jnp.dot(a_bf, lo, preferred_element_type=F32)
    return out


def _rwkv_chunk_kernel(r_ref, lw_ref, kx_ref, kk_ref, b_ref, v_ref, s0_ref, y_ref, sout_ref, s_scr, *, n_tb):
    tb = pl.program_id(1)
    C = RWKV_CHUNK
    R2 = 2 * C

    @pl.when(tb == 0)
    def _():
        s_scr[...] = s0_ref[0]

    ri = lax.broadcasted_iota(jnp.int32, (R2, R2), 0)
    ci = lax.broadcasted_iota(jnp.int32, (R2, R2), 1)
    same_head = (ri // C) == (ci // C)
    lower_strict = same_head & ((ri % C) > (ci % C))
    lower_incl = same_head & ((ri % C) >= (ci % C))
    eye = ri == ci
    same_base = (ri // INV_BASE) == (ci // INV_BASE)
    tri = jnp.where((lax.broadcasted_iota(jnp.int32, (C, C), 0) >= lax.broadcasted_iota(jnp.int32, (C, C), 1)),
                    1.0, 0.0).astype(BF16)
    lane = lax.broadcasted_iota(jnp.int32, (C, LANES), 1)
    left = lane < HALF
    lane_s = lax.broadcasted_iota(jnp.int32, (HALF, LANES), 1)
    left_s = lane_s < HALF

    def stack(x):
        return jnp.concatenate([jnp.where(left, x, 0.0), jnp.where(left, 0.0, x)], axis=0)

    def tile2(x):
        return jnp.concatenate([x, x], axis=0)

    slabs = range(RWKV_HEADS // 2)
    cols = [slice(p * LANES, (p + 1) * LANES) for p in slabs]
    each = lambda f, *xs: [f(*a) for a in zip(*xs)]
    r = [r_ref[0, :, cs] for cs in cols]
    lw = [lw_ref[0, :, cs] for cs in cols]
    kx = [kx_ref[0, :, cs] for cs in cols]
    kk = [kk_ref[0, :, cs] for cs in cols]
    b = [b_ref[0, :, cs] for cs in cols]
    v = [v_ref[0, :, cs] for cs in cols]
    cl = each(lambda x: _mm_exact_lhs(tri, x), lw)
    cl_end = each(lambda x: x[C - 1:C], cl)
    g_inv = each(lambda x: jnp.exp(-x), cl)
    g_end = each(lambda e, x: jnp.exp(e - x), cl_end, cl)
    kt = each(lambda k_, c_, l_: k_ * jnp.exp(c_ - l_), kk, cl, lw)
    rt = each(lambda r_, c_: r_ * jnp.exp(c_), r, cl)
    bh = each(jnp.multiply, b, g_inv)
    kh = each(jnp.multiply, kx, g_inv)
    bbar = each(jnp.multiply, b, g_end)
    kbar = each(jnp.multiply, kx, g_end)
    gram = each(lambda kt_, rt_, bh_, kh_: _mm3(jnp.concatenate([kt_, rt_], axis=0),
                                                jnp.concatenate([stack(bh_), stack(kh_)], axis=0), NT_DIMS),
                kt, rt, bh, kh)
    n_mat = each(lambda g_: jnp.where(lower_strict, tile2(g_[0:C, 0:R2]), 0.0), gram)
    ak = each(lambda g_: jnp.where(lower_strict, tile2(g_[0:C, R2:2 * R2]), 0.0), gram)
    bb = each(lambda g_: jnp.where(lower_incl, tile2(g_[C:R2, 0:R2]), 0.0), gram)
    bk = each(lambda g_: jnp.where(lower_incl, tile2(g_[C:R2, R2:2 * R2]), 0.0), gram)
    d1 = each(lambda n_: jnp.where(same_base, n_, 0.0), n_mat)
    d2 = each(_mm3, d1, d1)
    d4 = each(_mm3, d2, d2)
    d8 = each(_mm3, d4, d4)
    t_inv = each(lambda d_: jnp.where(eye, 1.0, 0.0) - d_, d1)
    for dk in (d2, d4, d8):
        t_inv = each(lambda t_, d_: t_ + _mm3(t_, d_), t_inv, dk)
    blk = INV_BASE
    while blk < C:
        off_mask = ((ri // (2 * blk)) == (ci // (2 * blk))) & ((ri // blk) != (ci // blk))
        tmp = each(lambda t_, n_: _mm3(t_, jnp.where(off_mask, n_, 0.0)), t_inv, n_mat)
        t_inv = each(lambda t_, m_: t_ - _mm3(m_, t_), t_inv, tmp)
        blk *= 2
    v_st = each(stack, v)
    kt_st = each(stack, kt)
    w_st = each(_mm3, ak, v_st)
    pq = each(lambda t_, k_, w_: _mm3(t_, jnp.concatenate([k_, w_], axis=1)), t_inv, kt_st, w_st)
    z = each(lambda bb_, bk_, pq_, v_: _mm3(
        jnp.concatenate([-bb_, bk_], axis=1),
        jnp.concatenate([pq_, jnp.concatenate([jnp.zeros_like(v_), v_], axis=1)], axis=0)), bb, bk, pq, v_st)
    ry_st = each(lambda rt_, z_: stack(rt_) + z_[:, 0:LANES], rt, z)
    bbar_st = each(stack, bbar)
    g_mat = each(lambda e_, pq_, bs_: jnp.where(eye, jnp.broadcast_to(jnp.exp(e_), (R2, LANES)), 0.0)
                 - _mm3(pq_[:, 0:LANES].T, bs_), cl_end, pq, bbar_st)
    h_bd = each(lambda v_, pq_, kb_, bs_: _mm3(jnp.concatenate([v_, pq_[:, LANES:2 * LANES]], axis=0).T,
                                               jnp.concatenate([stack(kb_), -bs_], axis=0)), v_st, pq, kbar, bbar_st)
    for p in slabs:
        s_old = s_scr[p]
        s_dup = jnp.concatenate([jnp.where(left_s, s_old, 0.0), jnp.where(left_s, 0.0, s_old)], axis=0)
        y_st = _mm3(ry_st[p], s_dup, NT_DIMS) + z[p][:, LANES:2 * LANES]
        y_ref[0, :, cols[p]] = y_st[0:C] + y_st[C:R2]
        s_scr[p] = _mm3(s_old, g_mat[p]) + (h_bd[p][0:HALF] + h_bd[p][HALF:LANES])

    @pl.when(tb == n_tb - 1)
    def _():
        sout_ref[0] = s_scr[...]


def rwkv_chunk_scan(r, lw, kx, kk, b, v, s0_slabs):
    bsz, t, W = r.shape
    n_tb = t // RWKV_CHUNK
    assert t % RWKV_CHUNK == 0
    seq = pl.BlockSpec((1, RWKV_CHUNK, W), lambda bi, ti: (bi, ti, 0))
    st = pl.BlockSpec((1, RWKV_HEADS // 2, 64, LANES), lambda bi, ti: (bi, 0, 0, 0))
    return pl.pallas_call(
        functools.partial(_rwkv_chunk_kernel, n_tb=n_tb),
        grid=(bsz, n_tb),
        in_specs=[seq] * 6 + [st],
        out_specs=[seq, st],
        out_shape=[jax.ShapeDtypeStruct((bsz, t, W), F32),
                   jax.ShapeDtypeStruct((bsz, RWKV_HEADS // 2, 64, LANES), F32)],
        scratch_shapes=[pltpu.VMEM((RWKV_HEADS // 2, 64, LANES), F32)],
        compiler_params=_cparams("parallel", "arbitrary"),
        name="rwkv_chunk_scan",
    )(r, lw, kx, kk, b, v, s0_slabs)


def _rwkv_post_kernel(y_ref, r_ref, kx_ref, v_ref, g_ref, rk_ref, lnw_ref, lnb_ref, bo_ref, o_ref):
    bo = bo_ref[...]
    y = y_ref[...]
    inv_n = 1.0 / RWKV_HEAD_SIZE
    ym = _head_sum(y, bo) * inv_n
    d = y - ym
    yv = _head_sum(d * d, bo) * inv_n
    yn = d * lax.rsqrt(yv + RWKV_GN_EPS) * lnw_ref[...] + lnb_ref[...]
    bonus = _head_sum(r_ref[...] * kx_ref[...] * rk_ref[...], bo) * v_ref[...]
    o_ref[...] = ((yn + bonus) * g_ref[...]).astype(BF16)


def rwkv_post(y, r, kx, v, g, r_k, ln_w, ln_b, bo, *, tm):
    m, W = y.shape
    tile = pl.BlockSpec((tm, W), lambda i: (i, 0))
    row = pl.BlockSpec((1, W), lambda i: (0, 0))
    return pl.pallas_call(
        _rwkv_post_kernel,
        grid=(m // tm,),
        in_specs=[tile] * 5 + [row] * 3 + [pl.BlockSpec(bo.shape, lambda i: (0, 0))],
        out_specs=tile,
        out_shape=jax.ShapeDtypeStruct((m, W), BF16),
        compiler_params=_cparams("parallel"),
        name="rwkv_post",
    )(y, r, kx, v, g, r_k, ln_w, ln_b, bo)


def rwkv_consts(mu, w0, w2, a0, a2, g2, k_k, k_a):
    W = RWKV_WIDTH
    tail = UA_PAD - 3 * W
    mu_p = jnp.zeros((1, UA_PAD), F32).at[0, :RWKV_COLS].set(mu.astype(F32))
    o_a = DECAY_LORA
    o_g = DECAY_LORA + AAA_LORA
    w2p = jnp.zeros((tail, W), BF16).at[0:o_a].set(w2.astype(BF16))
    a2p = jnp.zeros((tail, W), BF16).at[o_a:o_g].set(a2.astype(BF16))
    g2p = jnp.zeros((tail, W), BF16).at[o_g:o_g + GATE_LORA].set(g2.astype(BF16))
    half = jnp.arange(LANES) // RWKV_HEAD_SIZE
    bo = (half[:, None] == half[None, :]).astype(BF16)
    row = lambda p: p.astype(F32).reshape(1, W)
    return (mu_p, row(w0), row(a0), row(k_k), row(k_a), w2p, a2p, g2p, bo)


def _state_to_slabs(s):
    bsz = s.shape[0]
    hp = RWKV_HEADS // 2
    return s.reshape(bsz, hp, 2, 64, 64).transpose(0, 1, 3, 2, 4).reshape(bsz, hp, 64, LANES)


def _slabs_to_state(s):
    bsz = s.shape[0]
    hp = RWKV_HEADS // 2
    return s.reshape(bsz, hp, 64, 2, 64).transpose(0, 1, 3, 2, 4).reshape(bsz, RWKV_HEADS, 64, 64)


PREP_TM = 256


def rwkv_time_mix(u, m0, m, bsz, t, shift_state, s0, consts, r_k, ln_w, ln_b, *, prompt):
    W = RWKV_WIDTH
    bo = consts[-1]
    if prompt:
        tm = min(PREP_TM, m)
        nt = m // tm
        prev_rows = u[m0 + tm - 1:m0 + m - 1:tm, OFF_UA:OFF_UA + UA_PAD]
        rows = jnp.concatenate([jnp.zeros((1, UA_PAD), F32), prev_rows], axis=0)
        first = jnp.zeros((nt, SUBLANES, UA_PAD), F32).at[:, 0].set(rows).reshape(nt * SUBLANES, UA_PAD)
        period = tm
        s0_slabs = jnp.zeros((bsz, RWKV_HEADS // 2, 64, LANES), F32)
    else:
        tm = m
        sp = jnp.zeros((bsz, UA_PAD), F32).at[:, :RWKV_COLS].set(shift_state.astype(F32))
        first = jnp.repeat(sp, t, axis=0)
        period = t
        s0_slabs = _state_to_slabs(s0.astype(F32))
    r, lw, kx, v, kk, b, g = rwkv_prep(u, first, consts, m0=m0, m=m, tm=tm, period=period)
    tpad = -(-t // RWKV_CHUNK) * RWKV_CHUNK
    seq = lambda a: jnp.pad(a.reshape(bsz, t, W), ((0, 0), (0, tpad - t), (0, 0)))
    y3, s_fin = rwkv_chunk_scan(seq(r), seq(lw), seq(kx), seq(kk), seq(b), seq(v), s0_slabs)
    y = y3[:, :t].reshape(m, W)
    za = rwkv_post(y, r, kx, v, g, r_k.astype(F32).reshape(1, W), ln_w.astype(F32).reshape(1, W),
                   ln_b.astype(F32).reshape(1, W), bo, tm=tm)
    return za, _slabs_to_state(s_fin)


KV_ROW = 2 * NSA_KV_HEADS * HEAD_DIM
CHUNKS_PER_PAGE = PAGE_SIZE // CMP_STRIDE
PROJ_W = CMP_RANK * KV_ROW


PAGES_PER_STEP = 16


def _page_specs(block, n_pages):
    def spec(i):
        return pl.BlockSpec(block, lambda b, p, pt: (pt[b * n_pages + p * PAGES_PER_STEP + i], 0, 0))
    return [spec(i) for i in range(PAGES_PER_STEP)]


KV_SLOTS = 2 * NSA_KV_HEADS
PAGE_ROWS = PAGE_SIZE * KV_SLOTS


def _load_slot(page_ref, slot, n_tokens=PAGE_SIZE):
    return page_ref[0, pl.ds(slot, n_tokens, stride=KV_SLOTS), :]


def _cmp_proj_kernel(pt_ref, *refs):
    del pt_ref
    x_refs = refs[:PAGES_PER_STEP]
    w_ref, f_ref, s_ref, xb_scr = refs[PAGES_PER_STEP:]
    half = KV_ROW // 2
    n_out = CMP_RANK * CHUNKS_PER_PAGE
    for i, x_ref in enumerate(x_refs):
        for slot in range(KV_SLOTS):
            xb_scr[i, :, slot * HEAD_DIM:(slot + 1) * HEAD_DIM] = _load_slot(x_ref, slot).astype(BF16)
        rows = slice(i * CHUNKS_PER_PAGE, (i + 1) * CHUNKS_PER_PAGE)
        for s in range(2):
            out = jnp.dot(w_ref[s], xb_scr[i, :, s * half:(s + 1) * half], preferred_element_type=F32)
            for r in range(CMP_RANK):
                cols = slice(r * KV_ROW + s * half, r * KV_ROW + (s + 1) * half)
                f_ref[0, rows, cols] = out[r * CHUNKS_PER_PAGE:(r + 1) * CHUNKS_PER_PAGE]
                s_ref[0, rows, cols] = out[n_out + r * CHUNKS_PER_PAGE:n_out + (r + 1) * CHUNKS_PER_PAGE]


def cmp_project(pages, page_table_flat, bsz, n_pages, w_proj):
    assert n_pages % PAGES_PER_STEP == 0
    n_ch = n_pages * CHUNKS_PER_PAGE
    step_ch = PAGES_PER_STEP * CHUNKS_PER_PAGE
    out = jax.ShapeDtypeStruct((bsz, n_ch, PROJ_W), F32)
    o_spec = pl.BlockSpec((1, step_ch, PROJ_W), lambda b, p, pt: (b, p, 0))
    w_spec = pl.BlockSpec(w_proj.shape, lambda b, p, pt: (0, 0, 0))
    return pl.pallas_call(
        _cmp_proj_kernel,
        grid_spec=pltpu.PrefetchScalarGridSpec(
            num_scalar_prefetch=1,
            grid=(bsz, n_pages // PAGES_PER_STEP),
            in_specs=_page_specs((1, PAGE_ROWS, HEAD_DIM), n_pages) + [w_spec],
            out_specs=[o_spec, o_spec],
            scratch_shapes=[pltpu.VMEM((PAGES_PER_STEP, PAGE_SIZE, KV_ROW), BF16)],
        ),
        out_shape=[out, out],
        compiler_params=_cparams("parallel", "arbitrary"),
        name="cmp_project",
    )(page_table_flat, *([pages] * PAGES_PER_STEP), w_proj)


def _cmp_mix_kernel(f_ref, s_ref, waf_ref, was_ref, pex_ref, wb_ref, o_ref):
    n_ch = f_ref.shape[1]
    acc = jnp.zeros((n_ch, KV_ROW), F32)
    for r in range(CMP_RANK):
        rs = slice(r * CMP_STRIDE, (r + 1) * CMP_STRIDE)
        pe = jnp.sum(waf_ref[rs, :] * pex_ref[0:CMP_STRIDE, :], axis=0, keepdims=True)
        pe += jnp.sum(was_ref[rs, :] * pex_ref[CMP_STRIDE:CMP_BLOCK, :], axis=0, keepdims=True)
        cs = slice(r * KV_ROW, (r + 1) * KV_ROW)
        nxt = pltpu.roll(s_ref[0, :, cs], n_ch - 1, axis=0)
        hid = jax.nn.gelu(f_ref[0, :, cs] + nxt + pe)
        acc += jnp.dot(hid.astype(BF16), wb_ref[r], preferred_element_type=F32)
    o_ref[0] = acc.astype(BF16)


def cmp_mix(f, s, waf, was, pex, wb_bd):
    bsz, n_ch, _ = f.shape
    io = pl.BlockSpec((1, n_ch, PROJ_W), lambda b: (b, 0, 0))
    full2 = lambda a: pl.BlockSpec(a.shape, lambda b: (0, 0))
    return pl.pallas_call(
        _cmp_mix_kernel,
        grid=(bsz,),
        in_specs=[io, io, full2(waf), full2(was), full2(pex), pl.BlockSpec(wb_bd.shape, lambda b: (0, 0, 0))],
        out_specs=pl.BlockSpec((1, n_ch, KV_ROW), lambda b: (b, 0, 0)),
        out_shape=jax.ShapeDtypeStruct((bsz, n_ch, KV_ROW), BF16),
        compiler_params=_cparams("parallel"),
        name="cmp_mix",
    )(f, s, waf, was, pex, wb_bd)


def cmp_consts(cmp_pe, cmp_wa, cmp_wb):
    hd = NSA_KV_HEADS * HEAD_DIM
    wa_cols = jnp.repeat(cmp_wa.astype(F32).transpose(2, 1, 0), hd, axis=2)
    waf = wa_cols[:, :CMP_STRIDE].reshape(CMP_RANK * CMP_STRIDE, KV_ROW)
    was = wa_cols[:, CMP_STRIDE:].reshape(CMP_RANK * CMP_STRIDE, KV_ROW)
    pex = jnp.broadcast_to(cmp_pe.astype(F32).transpose(1, 0, 2)[:, :, None, :],
                           (CMP_BLOCK, 2, NSA_KV_HEADS, HEAD_DIM)).reshape(CMP_BLOCK, KV_ROW)
    eye = jnp.eye(NSA_KV_HEADS, dtype=F32)
    wb = cmp_wb.astype(F32)
    bd = jnp.einsum('srde,hg,st->rshdtge', wb, eye, jnp.eye(2, dtype=F32)).reshape(CMP_RANK, KV_ROW, KV_ROW)
    wa4 = cmp_wa.astype(F32).reshape(2, 2, CMP_STRIDE, CMP_RANK)
    w_proj = jnp.einsum('sflr,nm->sfrnml', wa4, jnp.eye(CHUNKS_PER_PAGE, dtype=F32)).reshape(
        2, 2 * CMP_RANK * CHUNKS_PER_PAGE, PAGE_SIZE)
    return waf, was, pex, bd.astype(BF16), w_proj.astype(BF16)


def compress(pages, page_table_flat, bsz, n_pages, consts):
    waf, was, pex, wb_bd, w_proj = consts
    f, s = cmp_project(pages, page_table_flat, bsz, n_pages, w_proj)
    kvc = cmp_mix(f, s, waf, was, pex, wb_bd)
    ratio = SEL_BLOCK // CMP_STRIDE
    n_ch = LANES * ratio
    kvc = jnp.pad(kvc, ((0, 0), (0, n_ch - kvc.shape[1]), (0, 0)))
    return kvc.reshape(bsz, LANES, ratio, KV_ROW).transpose(0, 2, 1, 3).reshape(bsz, n_ch, KV_ROW)


NT_DIMS = (((1,), (1,)), ((), ()))
HALF = LANES // 2
KEY_TILE = 256
WIN_SPAN = WINDOW + Q_BLOCK
TOPK_UNROLL = 8


def _masked_softmax_rows(s, ok):
    x = jnp.where(ok, s, NEG_INF)
    m = jnp.max(x, axis=-1, keepdims=True)
    e = jnp.exp(x - m)
    l = jnp.sum(e, axis=-1, keepdims=True)
    return jnp.where(ok, e / l, 0.0)


def _top_blocks(score, n_keep):
    lane = lax.broadcasted_iota(jnp.int32, score.shape, 1)

    def body(d, cnt):
        rolled = pltpu.roll(score, d, axis=1)
        beats = (rolled > score) | ((rolled == score) & (lane >= d))
        return cnt + jnp.where(beats, 1.0, 0.0)

    cnt = lax.fori_loop(1, LANES, body, jnp.zeros(score.shape, F32), unroll=TOPK_UNROLL)
    return cnt < n_keep


def _rank_unselected(st_ref, n_keep):
    n = st_ref.shape[1]
    sub = lax.broadcasted_iota(jnp.int32, (SUBLANES, LANES), 0)
    cols = []
    for c in range(n // LANES):
        cs = slice(c * LANES, (c + 1) * LANES)
        tiles = [st_ref[t * SUBLANES:(t + 1) * SUBLANES, cs] for t in range(LANES // SUBLANES)]
        cnts = [jnp.zeros((SUBLANES, LANES), F32) for _ in tiles]
        for j in range(LANES):
            sj = st_ref[j:j + 1, cs]
            for t, si in enumerate(tiles):
                if t * SUBLANES > j:
                    beats = sj >= si
                elif (t + 1) * SUBLANES <= j:
                    beats = sj > si
                else:
                    beats = (sj > si) | ((sj == si) & (sub > j - t * SUBLANES))
                cnts[t] = cnts[t] + jnp.where(beats, 1.0, 0.0)
        cols.append(jnp.concatenate(cnts, axis=0))
    return jnp.where(jnp.concatenate(cols, axis=1) < n_keep, 0.0, 1.0).astype(BF16)


def _col_softmax(x):
    m = jnp.max(x, axis=0, keepdims=True)
    e = jnp.exp(x - m)
    l = jnp.sum(e, axis=0, keepdims=True)
    return e * (1.0 / l), m, l


def _nsa_prompt_kernel(q_ref, gn_ref, kc_ref, vct_ref, ks_ref, vst_ref, kw_ref, vwt_ref, o_ref, st_scr):
    i = pl.program_id(0)
    start = i * Q_BLOCK
    assert math.log2(ATTN_SCALE).is_integer()
    q = q_ref[...] * ATTN_SCALE
    gates_t = jax.nn.sigmoid(gn_ref[...]).T
    pos = start + lax.broadcasted_iota(jnp.int32, (1, Q_BLOCK), 1)
    n_cmp = kc_ref.shape[0]
    ratio = SEL_BLOCK // CMP_STRIDE
    crow = lax.broadcasted_iota(jnp.int32, (n_cmp, 1), 0)
    cmp_idx = (crow % LANES) * ratio + crow // LANES
    cmp_ok = (cmp_idx * CMP_STRIDE + (CMP_BLOCK - 1)) <= pos
    cmp_bias = jnp.where(cmp_ok, 0.0, NEG_INF)
    blk = lax.broadcasted_iota(jnp.int32, (LANES, 1), 0)
    cur = pos // SEL_BLOCK
    forced = (blk == 0) | (blk == cur) | (blk == cur - 1)
    future = blk * SEL_BLOCK > pos
    key_sub = lax.broadcasted_iota(jnp.int32, (KEY_TILE, 1), 0)
    blk_lane = lax.broadcasted_iota(jnp.int32, (1, LANES), 1)
    lane_q = lax.broadcasted_iota(jnp.int32, (Q_BLOCK, LANES), 1)
    n_tiles = start // KEY_TILE + 1
    wb0 = jnp.maximum(start - WINDOW, 0) // Q_BLOCK
    n_wblk = WIN_SPAN // Q_BLOCK

    heads = range(NSA_KV_HEADS)
    groups = range(NSA_GROUP)
    ksls = [slice((h // 2) * LANES, (h // 2 + 1) * LANES) for h in heads]
    keeps = [(lane_q // HALF) == (h % 2) for h in heads]
    qs = []
    for h in heads:
        row = []
        for g in groups:
            head = h * NSA_GROUP + g
            slab = q[:, (head // 2) * LANES:(head // 2 + 1) * LANES]
            if head % 2 != h % 2:
                slab = pltpu.roll(slab, HALF, axis=1)
            row.append(jnp.where(keeps[h], slab, 0.0).astype(BF16))
        qs.append(row)

    pairs = [(h, g) for h in heads for g in groups]
    cs_ = [lax.dot_general(kc_ref[:, ksls[h]], qs[h][g], NT_DIMS, preferred_element_type=F32) + cmp_bias
           for h, g in pairs]
    pcs = [jnp.where(cmp_ok, _col_softmax(x)[0], 0.0) for x in cs_]
    o_c = [jnp.dot(vct_ref[ksls[h], :], pcs[i].astype(BF16), preferred_element_type=F32)
           for i, (h, g) in enumerate(pairs)]
    for h in heads:
        imp = pcs[h * NSA_GROUP]
        for g in range(1, NSA_GROUP):
            imp = imp + pcs[h * NSA_GROUP + g]
        imp_blk = imp[0:LANES]
        for j in range(1, ratio):
            imp_blk = imp_blk + imp[j * LANES:(j + 1) * LANES]
        st_scr[:, h * Q_BLOCK:(h + 1) * Q_BLOCK] = jnp.where(future, -FORCE, jnp.where(forced, FORCE, imp_blk))
    unsel = _rank_unselected(st_scr, N_SEL)

    def sel_tile(kt, carry, causal):
        k0 = kt * KEY_TILE
        expand = jnp.where((k0 + key_sub) // SEL_BLOCK == blk_lane, NEG_INF, 0.0).astype(BF16)
        biases = [jnp.dot(expand, unsel[:, h * Q_BLOCK:(h + 1) * Q_BLOCK], preferred_element_type=F32) for h in heads]
        if causal:
            visible = (k0 + key_sub) <= pos
            biases = [jnp.where(visible, b_, NEG_INF) for b_ in biases]
        pairs = [(h, g) for h in heads for g in groups]
        xs = [lax.dot_general(ks_ref[kt, :, ksls[h]], qs[h][g], NT_DIMS, preferred_element_type=F32) + biases[h]
              for h, g in pairs]
        m_new = [jnp.maximum(carry[i][0], jnp.max(x, axis=0, keepdims=True)) for i, x in enumerate(xs)]
        ps = [jnp.exp(x - m) for x, m in zip(xs, m_new)]
        scs = [jnp.exp(carry[i][0] - m) for i, m in enumerate(m_new)]
        l_new = [sc * carry[i][1] + jnp.sum(p, axis=0, keepdims=True) for i, (sc, p) in enumerate(zip(scs, ps))]
        pvs = [jnp.dot(vst_ref[kt, ksls[h], :], ps[i].astype(BF16), preferred_element_type=F32)
               for i, (h, g) in enumerate(pairs)]
        return tuple((m_new[i], l_new[i], scs[i] * carry[i][2] + pvs[i]) for i in range(len(pairs)))

    init = tuple((jnp.full((1, Q_BLOCK), NEG_INF, F32), jnp.zeros((1, Q_BLOCK), F32), jnp.zeros((LANES, Q_BLOCK), F32))
                 for _ in range(NSA_HEADS))
    carry = lax.fori_loop(0, n_tiles - 1, functools.partial(sel_tile, causal=False), init)
    carry = sel_tile(n_tiles - 1, carry, True)

    kpos = wb0 * Q_BLOCK + lax.broadcasted_iota(jnp.int32, (WIN_SPAN, 1), 0)
    delta = pos - kpos
    win_bias = jnp.where((delta >= 0) & (delta < WINDOW), 0.0, NEG_INF)
    k_ws = [jnp.concatenate([kw_ref[wb0 + j, :, ksls[h]] for j in range(n_wblk)], axis=0) for h in heads]
    ws_ = [lax.dot_general(k_ws[h], qs[h][g], NT_DIMS, preferred_element_type=F32) + win_bias for h, g in pairs]
    pws = [_col_softmax(x)[0].astype(BF16) for x in ws_]
    o_w = []
    for i, (h, g) in enumerate(pairs):
        acc = jnp.dot(vwt_ref[wb0, ksls[h], :], pws[i][0:Q_BLOCK], preferred_element_type=F32)
        for j in range(1, n_wblk):
            acc += jnp.dot(vwt_ref[wb0 + j, ksls[h], :], pws[i][j * Q_BLOCK:(j + 1) * Q_BLOCK],
                           preferred_element_type=F32)
        o_w.append(acc)
    pieces = []
    for head, (h, g) in enumerate(pairs):
        _, l_s, acc_s = carry[head]
        o_s = acc_s * (1.0 / l_s)
        mix = (gates_t[head:head + 1] * o_c[head] + gates_t[NSA_HEADS + head:NSA_HEADS + head + 1] * o_s
               + gates_t[2 * NSA_HEADS + head:2 * NSA_HEADS + head + 1] * o_w[head])
        pieces.append(mix[(h % 2) * HALF:(h % 2 + 1) * HALF])
    o_t = jnp.concatenate(pieces, axis=0)
    o_ref[...] = o_t.T.astype(BF16)


def nsa_prompt(u, t, kvc, sel_kv_bf, win_kv_bf):
    n_qb = t // Q_BLOCK
    hw = NSA_KV_HEADS * HEAD_DIM
    tiles = lambda a, rows: a.reshape(t // rows, rows, hw)
    tiles_t = lambda a, rows: a.reshape(t // rows, rows, hw).transpose(0, 2, 1)
    operands = [
        kvc[:, :hw], kvc[:, hw:].T,
        tiles(sel_kv_bf[:, :hw], KEY_TILE), tiles_t(sel_kv_bf[:, hw:], KEY_TILE),
        tiles(win_kv_bf[:, :hw], Q_BLOCK), tiles_t(win_kv_bf[:, hw:], Q_BLOCK),
    ]
    whole = lambda a: pl.BlockSpec(a.shape, lambda i: (0,) * a.ndim)
    return pl.pallas_call(
        _nsa_prompt_kernel,
        grid=(n_qb,),
        in_specs=[
            pl.BlockSpec((Q_BLOCK, Q_COLS), lambda i: (i, OFF_Q // Q_COLS)),
            pl.BlockSpec((Q_BLOCK, GN_PAD), lambda i: (i, OFF_GN // GN_PAD)),
        ] + [whole(a) for a in operands],
        out_specs=pl.BlockSpec((Q_BLOCK, NSA_WIDTH), lambda i: (i, 0)),
        out_shape=jax.ShapeDtypeStruct((t, NSA_WIDTH), BF16),
        scratch_shapes=[pltpu.VMEM((LANES, NSA_KV_HEADS * Q_BLOCK), F32)],
        compiler_params=_cparams("arbitrary"),
        name="nsa_prompt",
    )(u, u, *operands)


def _tail_keys(qf, k_new, v_new, tok, m, l, acc):
    nt = k_new.shape[0]
    kn = k_new.astype(BF16).astype(F32)
    vn = v_new.astype(BF16).astype(F32)
    s = [jnp.sum(qf * kn[j:j + 1], axis=1, keepdims=True) * ATTN_SCALE for j in range(nt)]
    ok = [tok >= j for j in range(nt)]
    m_new = m
    for j in range(nt):
        m_new = jnp.maximum(m_new, jnp.where(ok[j], s[j], NEG_INF))
    scale = jnp.exp(m - m_new)
    l = scale * l
    acc = scale * acc
    for j in range(nt):
        p = jnp.where(ok[j], jnp.exp(s[j] - m_new), 0.0)
        l = l + p
        acc = acc + p * vn[j:j + 1]
    return m_new, l, acc


def _nsa_sample_kernel(pt_ref, q_ref, gn_ref, kc_ref, vc_ref, *refs, nt, past, n_steps):
    del pt_ref
    page_refs = refs[:PAGES_PER_STEP]
    (snew_ref, wnew_ref, win_ref, o_ref,
     qh_scr, sel_scr, oc_scr, m_scr, l_scr, acc_scr, kb_scr, vb_scr, kw_scr, vw_scr) = refs[PAGES_PER_STEP:]
    p = pl.program_id(1)
    rows = NSA_GROUP * nt
    hw = NSA_KV_HEADS * HEAD_DIM
    row1 = lax.broadcasted_iota(jnp.int32, (rows, 1), 0)
    tok = row1 % nt
    grp = row1 // nt
    pos = past + tok
    lane = lax.broadcasted_iota(jnp.int32, (rows, LANES), 1)
    blk = lax.broadcasted_iota(jnp.int32, (1, LANES), 1)
    ratio = SEL_BLOCK // CMP_STRIDE

    @pl.when(p == 0)
    def _():
        q16 = q_ref[0]
        n_cmp = kc_ref.shape[1]
        col = lax.broadcasted_iota(jnp.int32, (1, n_cmp), 1)
        cmp_idx = (col % LANES) * ratio + col // LANES
        cmp_ok = (cmp_idx * CMP_STRIDE + (CMP_BLOCK - 1)) <= pos
        cur = pos // SEL_BLOCK
        forced = (blk == 0) | (blk == cur) | (blk == cur - 1)
        future = blk * SEL_BLOCK > pos
        for h in range(NSA_KV_HEADS):
            khalf = h % 2
            ksl = slice((h // 2) * LANES, (h // 2 + 1) * LANES)
            keep = (lane // HALF) == khalf
            qh = jnp.zeros((rows, LANES), F32)
            for g in range(NSA_GROUP):
                head = h * NSA_GROUP + g
                slab = q16[:, (head // 2) * LANES:(head // 2 + 1) * LANES]
                if head % 2 != khalf:
                    slab = pltpu.roll(slab, HALF, axis=1)
                qh = jnp.where((grp == g) & keep, slab, qh)
            qh_scr[h] = qh
            sc = lax.dot_general(qh.astype(BF16), kc_ref[0, :, ksl], NT_DIMS, preferred_element_type=F32) * ATTN_SCALE
            pc = _masked_softmax_rows(sc, cmp_ok)
            oc_scr[h] = jnp.dot(pc.astype(BF16), vc_ref[0, :, ksl], preferred_element_type=F32)
            imp = pc
            for g in range(1, NSA_GROUP):
                imp = imp + pltpu.roll(pc, g * nt, axis=0)
            imp_blk = imp[:, 0:LANES]
            for j in range(1, ratio):
                imp_blk = imp_blk + imp[:, j * LANES:(j + 1) * LANES]
            score = jnp.where(future, -FORCE, jnp.where(forced, FORCE, imp_blk))
            sel_scr[h] = jnp.where(_top_blocks(score, N_SEL - 1), 1.0, 0.0)
            m_scr[h] = jnp.full((rows, 1), NEG_INF, F32)
            l_scr[h] = jnp.zeros((rows, 1), F32)
            acc_scr[h] = jnp.zeros((rows, LANES), F32)

    step_keys = PAGES_PER_STEP * PAGE_SIZE
    for i, page_ref in enumerate(page_refs):
        ks = slice(i * PAGE_SIZE, (i + 1) * PAGE_SIZE)
        for h in range(NSA_KV_HEADS):
            hs_ = slice(h * HEAD_DIM, (h + 1) * HEAD_DIM)
            kb_scr[ks, hs_] = _load_slot(page_ref, h).astype(BF16)
            vb_scr[ks, hs_] = _load_slot(page_ref, NSA_KV_HEADS + h).astype(BF16)
    blk_row = lax.broadcasted_iota(jnp.int32, (LANES, 1), 0)
    key_lane = lax.broadcasted_iota(jnp.int32, (1, step_keys), 1)
    expand = jnp.where(blk_row == (p * step_keys + key_lane) // SEL_BLOCK, 1.0, 0.0).astype(BF16)
    for h in range(NSA_KV_HEADS):
        ksl = slice((h // 2) * LANES, (h // 2 + 1) * LANES)
        s = lax.dot_general(qh_scr[h].astype(BF16), kb_scr[:, ksl], NT_DIMS, preferred_element_type=F32) * ATTN_SCALE
        ok = jnp.dot(sel_scr[h].astype(BF16), expand, preferred_element_type=F32) > 0.5
        x = jnp.where(ok, s, NEG_INF)
        m_i = m_scr[h]
        m_new = jnp.maximum(m_i, jnp.max(x, axis=-1, keepdims=True))
        pr = jnp.where(ok, jnp.exp(x - m_new), 0.0)
        scale = jnp.exp(m_i - m_new)
        l_scr[h] = scale * l_scr[h] + jnp.sum(pr, axis=-1, keepdims=True)
        acc_scr[h] = scale * acc_scr[h] + jnp.dot(pr.astype(BF16), vb_scr[:, ksl], preferred_element_type=F32)
        m_scr[h] = m_new

    @pl.when(p == n_steps - 1)
    def _():
        gates = jax.nn.sigmoid(gn_ref[0])
        snew = snew_ref[0]
        wnew = wnew_ref[0]
        wrows = kw_scr.shape[0]
        for h in range(NSA_KV_HEADS):
            hs_ = slice(h * HEAD_DIM, (h + 1) * HEAD_DIM)
            kw_scr[:, hs_] = _load_slot(win_ref, h, wrows).astype(BF16)
            vw_scr[:, hs_] = _load_slot(win_ref, NSA_KV_HEADS + h, wrows).astype(BF16)
        kwb = kw_scr[...]
        vwb = vw_scr[...]
        widx = lax.broadcasted_iota(jnp.int32, (1, wrows), 1)
        win_ok = widx > tok + (wrows - WINDOW)
        stacks = []
        for h in range(NSA_KV_HEADS):
            ksl = slice((h // 2) * LANES, (h // 2 + 1) * LANES)
            vsl = slice(hw + (h // 2) * LANES, hw + (h // 2 + 1) * LANES)
            qb = qh_scr[h].astype(BF16)
            qf = qb.astype(F32)
            _, l_s, acc_s = _tail_keys(qf, snew[:, ksl], snew[:, vsl], tok, m_scr[h], l_scr[h], acc_scr[h])
            o_s = acc_s / l_s
            sw = lax.dot_general(qb, kwb[:, ksl], NT_DIMS, preferred_element_type=F32) * ATTN_SCALE
            x = jnp.where(win_ok, sw, NEG_INF)
            m_w = jnp.max(x, axis=-1, keepdims=True)
            pw = jnp.where(win_ok, jnp.exp(x - m_w), 0.0)
            l_w = jnp.sum(pw, axis=-1, keepdims=True)
            acc_w = jnp.dot(pw.astype(BF16), vwb[:, ksl], preferred_element_type=F32)
            _, l_w, acc_w = _tail_keys(qf, wnew[:, ksl], wnew[:, vsl], tok, m_w, l_w, acc_w)
            o_w = acc_w / l_w
            gc = jnp.zeros((rows, 1), F32)
            gs = jnp.zeros((rows, 1), F32)
            gw = jnp.zeros((rows, 1), F32)
            for g in range(NSA_GROUP):
                head = h * NSA_GROUP + g
                gc = jnp.where(grp == g, gates[:, head:head + 1], gc)
                gs = jnp.where(grp == g, gates[:, NSA_HEADS + head:NSA_HEADS + head + 1], gs)
                gw = jnp.where(grp == g, gates[:, 2 * NSA_HEADS + head:2 * NSA_HEADS + head + 1], gw)
            stacks.append(gc * oc_scr[h] + gs * o_s + gw * o_w)
        for sidx in range(NSA_HEADS // 2):
            h = sidx // 2
            halves = []
            for g in (2 * (sidx % 2), 2 * (sidx % 2) + 1):
                part = stacks[h]
                if g > 0:
                    part = pltpu.roll(part, rows - g * nt, axis=0)
                if g % 2 != h % 2:
                    part = pltpu.roll(part, HALF, axis=1)
                halves.append(part)
            slab = jnp.where(lane < HALF, halves[0], halves[1])
            o_ref[0, :, sidx * LANES:(sidx + 1) * LANES] = slab[0:nt]


def nsa_sample(q16, gn16, kvc, sel_pages, page_table_flat, snew, wnew, win_state, *, nt, past):
    bsz, rows, _ = q16.shape
    n_pages = past // PAGE_SIZE
    assert past % SEL_BLOCK == 0 and nt <= SEL_BLOCK and rows == NSA_GROUP * nt
    assert n_pages % PAGES_PER_STEP == 0
    n_steps = n_pages // PAGES_PER_STEP
    step_keys = PAGES_PER_STEP * PAGE_SIZE
    hw = NSA_KV_HEADS * HEAD_DIM
    per_b = lambda shape, cb=0: pl.BlockSpec((1,) + shape, lambda b, p, pt: (b, 0, cb))
    hs = NSA_KV_HEADS
    return pl.pallas_call(
        functools.partial(_nsa_sample_kernel, nt=nt, past=past, n_steps=n_steps),
        grid_spec=pltpu.PrefetchScalarGridSpec(
            num_scalar_prefetch=1,
            grid=(bsz, n_steps),
            in_specs=[
                per_b((rows, Q_COLS)), per_b((rows, GN_PAD)),
                per_b((kvc.shape[1], hw), 0), per_b((kvc.shape[1], hw), 1),
                *_page_specs((1, PAGE_ROWS, HEAD_DIM), n_pages),
                per_b((nt, KV_ROW)), per_b((nt, KV_ROW)),
                per_b((win_state.shape[1], HEAD_DIM)),
            ],
            out_specs=pl.BlockSpec((1, nt, NSA_WIDTH), lambda b, p, pt: (b, 0, 0)),
            scratch_shapes=[
                pltpu.VMEM((hs, rows, LANES), F32), pltpu.VMEM((hs, rows, LANES), F32),
                pltpu.VMEM((hs, rows, LANES), F32), pltpu.VMEM((hs, rows, 1), F32),
                pltpu.VMEM((hs, rows, 1), F32), pltpu.VMEM((hs, rows, LANES), F32),
                pltpu.VMEM((step_keys, hw), BF16), pltpu.VMEM((step_keys, hw), BF16),
                pltpu.VMEM((win_state.shape[1] // KV_SLOTS, hw), BF16),
                pltpu.VMEM((win_state.shape[1] // KV_SLOTS, hw), BF16),
            ],
        ),
        out_shape=jax.ShapeDtypeStruct((bsz, nt, NSA_WIDTH), F32),
        compiler_params=_cparams("parallel", "arbitrary"),
        name="nsa_sample",
    )(page_table_flat, q16, gn16, kvc, kvc, *([sel_pages] * PAGES_PER_STEP), snew, wnew, win_state)


def _merge_kernel(za_ref, ob_ref, g0_ref, g1_ref, h_ref, wua_ref, wub_ref, wo_ref, g_ref, b_ref, o_ref, obf_ref):
    ya = jnp.dot(za_ref[...], wua_ref[...], preferred_element_type=F32)
    yb = jnp.dot(ob_ref[...], wub_ref[...], preferred_element_type=F32)
    mixed = jax.nn.sigmoid(g0_ref[...]) * ya + jax.nn.sigmoid(g1_ref[...]) * yb
    mix = jnp.dot(mixed.astype(BF16), wo_ref[...], preferred_element_type=F32)
    o = _layer_norm(DEEPNORM_ALPHA * h_ref[...] + mix, g_ref[...], b_ref[...])
    o_ref[...] = o
    obf_ref[...] = o.astype(BF16)


def merge_ln(za, ob, u, h, wua, wub, wo, g, b, *, tm):
    m, d = h.shape
    W = za.shape[1]
    assert m % tm == 0
    tile = lambda w, cb=0: pl.BlockSpec((tm, w), lambda i: (i, cb))
    const = lambda a: pl.BlockSpec(a.shape, lambda i: (0, 0), pipeline_mode=pl.Buffered(1))
    row = pl.BlockSpec((1, d), lambda i: (0, 0))
    return pl.pallas_call(
        _merge_kernel,
        grid=(m // tm,),
        in_specs=[tile(W), tile(W), tile(d, OFF_GM // d), tile(d, OFF_GM // d + 1), tile(d),
                  const(wua), const(wub), const(wo), row, row],
        out_specs=[tile(d), tile(d)],
        out_shape=[jax.ShapeDtypeStruct((m, d), F32), jax.ShapeDtypeStruct((m, d), BF16)],
        compiler_params=_cparams("parallel"),
        name="merge_ln",
    )(za, ob, u, u, h, wua, wub, wo, g.reshape(1, d), b.reshape(1, d))


TOKEN_TM = 640
PROJ_TN = 2176
MERGE_TM = 320


def kernel(x_prompt, x_sample, cache_cmp_kv, cache_sel_kv, page_table, state_win_kv, state_rwkv, state_rwkv_shift,
           ln1_g, ln1_b, ffn1_w_gu, ffn1_w_down, w_in, rwkv_mu, rwkv_w0, rwkv_w2, rwkv_a0, rwkv_a2, rwkv_g2,
           rwkv_k_k, rwkv_k_a, rwkv_r_k, rwkv_ln_w, rwkv_ln_b, w_up_a, cmp_pe, cmp_wa, cmp_wb, w_up_b, w_o,
           ln2_g, ln2_b, ffn2_w_gu, ffn2_w_down, ln3_g, ln3_b):
    bp, tp, d = x_prompt.shape
    bs, ts, _ = x_sample.shape
    assert bp == 1
    mp, ms = bp * tp, bs * ts
    n_pool = cache_cmp_kv.shape[0]
    n_pages = page_table.shape[1]
    past = n_pages * PAGE_SIZE
    kvh = (2, NSA_KV_HEADS, HEAD_DIM)
    bf = lambda a: a.astype(BF16)
    f32 = lambda a: a.astype(F32)

    x = jnp.concatenate([f32(x_prompt).reshape(mp, d), f32(x_sample).reshape(ms, d)], axis=0)
    h1, h1_bf = ffn_ln(x, bf(ffn1_w_gu), bf(ffn1_w_down), f32(ln1_g), f32(ln1_b), tm=TOKEN_TM)
    u = matmul(h1_bf, bf(pad_in_cols(w_in)), tm=TOKEN_TM, tn=PROJ_TN)
    u_s = u[mp:].reshape(bs, ts, IN_PAD)
    kv_p = u[:mp, OFF_KV:OFF_KV + KV_COLS]
    cmp_p, sel_p, win_p = kv_p[:, 0:KV_ROW], kv_p[:, KV_ROW:2 * KV_ROW], kv_p[:, 2 * KV_ROW:3 * KV_ROW]
    cmp_s = u_s[:, :, OFF_KV:OFF_KV + KV_ROW]
    sel_s = u_s[:, :, OFF_KV + KV_ROW:OFF_KV + 2 * KV_ROW]
    win_s = u_s[:, :, OFF_KV + 2 * KV_ROW:OFF_KV + 3 * KV_ROW]

    rc = rwkv_consts(rwkv_mu, rwkv_w0, rwkv_w2, rwkv_a0, rwkv_a2, rwkv_g2, rwkv_k_k, rwkv_k_a)
    za_p, s_p = rwkv_time_mix(u, 0, mp, bp, tp, None, None, rc, rwkv_r_k, rwkv_ln_w, rwkv_ln_b, prompt=True)
    za_s, s_s = rwkv_time_mix(u, mp, ms, bs, ts, state_rwkv_shift, state_rwkv, rc, rwkv_r_k, rwkv_ln_w, rwkv_ln_b,
                              prompt=False)

    cc = cmp_consts(cmp_pe, cmp_wa, cmp_wb)
    as_pages = lambda a: f32(a).reshape(-1, PAGE_ROWS, HEAD_DIM)
    kvc_p = compress(as_pages(cmp_p), jnp.arange(mp // PAGE_SIZE, dtype=jnp.int32), 1, mp // PAGE_SIZE, cc)[0]
    o_p = nsa_prompt(u, mp, kvc_p, bf(sel_p), bf(win_p))
    pt_flat = page_table.reshape(-1).astype(jnp.int32)
    kvc_s = compress(as_pages(cache_cmp_kv), pt_flat, bs, n_pages, cc)
    q16 = jnp.tile(u_s[:, :, OFF_Q:OFF_Q + Q_COLS], (1, NSA_GROUP, 1))
    gn16 = jnp.tile(u_s[:, :, OFF_GN:OFF_GN + GN_PAD], (1, NSA_GROUP, 1))
    o_s = nsa_sample(q16, gn16, kvc_s, as_pages(cache_sel_kv), pt_flat, sel_s, win_s,
                     f32(state_win_kv).reshape(bs, -1, HEAD_DIM), nt=ts, past=past)

    za = jnp.concatenate([za_p, za_s], axis=0)
    ob = jnp.concatenate([o_p, bf(o_s.reshape(ms, NSA_WIDTH))], axis=0)
    h2, _ = merge_ln(za, ob, u, h1, bf(w_up_a), bf(w_up_b), bf(w_o), f32(ln2_g), f32(ln2_b), tm=MERGE_TM)
    y, _ = ffn_ln(h2, bf(ffn2_w_gu), bf(ffn2_w_down), f32(ln3_g), f32(ln3_b), tm=TOKEN_TM)

    wb = min(WINDOW, tp)
    wkeep = min(WINDOW, state_win_kv.shape[1] + ts)
    win_all = jnp.concatenate([state_win_kv, win_s.reshape(bs, ts, *kvh).astype(state_win_kv.dtype)], axis=1)
    cd, sd, wd = cache_cmp_kv.dtype, cache_sel_kv.dtype, state_win_kv.dtype
    rd, hd = state_rwkv.dtype, state_rwkv_shift.dtype
    return (
        y[:mp].reshape(bp, tp, d).astype(x_prompt.dtype),
        y[mp:].reshape(bs, ts, d).astype(x_sample.dtype),
        cmp_p.reshape(bp, tp, *kvh).astype(cd),
        sel_p.reshape(bp, tp, *kvh).astype(sd),
        win_p[tp - wb:].reshape(bp, wb, *kvh).astype(wd),
        s_p.astype(rd),
        u[mp - 1:mp, OFF_UA:OFF_UA + RWKV_COLS].astype(hd),
        cmp_s.reshape(bs, ts, *kvh).astype(cd),
        sel_s.reshape(bs, ts, *kvh).astype(sd),
        win_all[:, win_all.shape[1] - wkeep:].astype(wd),
        s_s.astype(rd),
        u_s[:, ts - 1, OFF_UA:OFF_UA + RWKV_COLS].astype(hd),
    )
```

```python
import functools
import math

import jax
import jax.numpy as jnp
from jax import lax
from jax.experimental import pallas as pl
from jax.experimental.pallas import tpu as pltpu

F32 = jnp.float32
BF16 = jnp.bfloat16

D_MODEL = 2048
RWKV_WIDTH = D_MODEL // 2
RWKV_HEAD_SIZE = 64
RWKV_HEADS = RWKV_WIDTH // RWKV_HEAD_SIZE
DECAY_LORA = 64
AAA_LORA = 64
GATE_LORA = 160
RWKV_GN_EPS = 64e-5
NSA_WIDTH = D_MODEL // 2
HEAD_DIM = 64
NSA_HEADS = NSA_WIDTH // HEAD_DIM
NSA_KV_HEADS = 4
NSA_GROUP = NSA_HEADS // NSA_KV_HEADS
CMP_STRIDE = 16
CMP_BLOCK = 2 * CMP_STRIDE
CMP_RANK = 4
SEL_BLOCK = 64
N_SEL = 16
WINDOW = 512
Q_BLOCK = 128
PAGE_SIZE = 128
ATTN_SCALE = HEAD_DIM ** -0.5
FFN_HIDDEN = 256 * math.ceil(8 * D_MODEL / 3 / 256)
DEPTH = 1
DEEPNORM_ALPHA = (2 * DEPTH) ** 0.25
LN_EPS = 1e-5
NEG_INF = -1e30
FORCE = 1e9
RWKV_COLS = 3 * RWKV_WIDTH + DECAY_LORA + AAA_LORA + GATE_LORA
Q_COLS = NSA_HEADS * HEAD_DIM
KV_COLS = 3 * 2 * NSA_KV_HEADS * HEAD_DIM
NSA_GATE_COLS = 3 * NSA_HEADS
MERGE_GATE_COLS = 2 * D_MODEL

LANES = 128
SUBLANES = 8
VMEM_LIMIT_BYTES = 56 * 1024 * 1024

UA_PAD = 3584
GN_PAD = 128
OFF_GM = 0
OFF_Q = OFF_GM + MERGE_GATE_COLS
OFF_KV = OFF_Q + Q_COLS
OFF_UA = 2 * UA_PAD
OFF_GN = OFF_UA + UA_PAD
IN_PAD = OFF_GN + GN_PAD


def pad_in_cols(w):
    o = 0
    ua = w[..., o:o + RWKV_COLS]; o += RWKV_COLS
    q = w[..., o:o + Q_COLS]; o += Q_COLS
    kv = w[..., o:o + KV_COLS]; o += KV_COLS
    gn = w[..., o:o + NSA_GATE_COLS]; o += NSA_GATE_COLS
    gm = w[..., o:o + MERGE_GATE_COLS]
    z = lambda n: jnp.zeros(w.shape[:-1] + (n,), w.dtype)
    return jnp.concatenate([gm, q, kv, z(OFF_UA - (OFF_KV + KV_COLS)), ua, z(UA_PAD - RWKV_COLS), gn,
                            z(GN_PAD - NSA_GATE_COLS)], axis=-1)


def _cparams(*sem):
    return pltpu.CompilerParams(dimension_semantics=sem, vmem_limit_bytes=VMEM_LIMIT_BYTES)


def _layer_norm(y, g, b):
    mu = jnp.mean(y, axis=-1, keepdims=True)
    d = y - mu
    var = jnp.mean(d * d, axis=-1, keepdims=True)
    return d * lax.rsqrt(var + LN_EPS) * g + b


def _ffn_ln_kernel(x_ref, wg_ref, wu_ref, wd_ref, g_ref, b_ref, o_ref, ob_ref, xb_ref, acc_ref, *, n_chunks):
    j = pl.program_id(1)

    @pl.when(j == 0)
    def _():
        xb_ref[...] = x_ref[...].astype(BF16)
        acc_ref[...] = jnp.zeros_like(acc_ref)

    xb = xb_ref[...]
    gate = jnp.dot(xb, wg_ref[...], preferred_element_type=F32)
    up = jnp.dot(xb, wu_ref[...], preferred_element_type=F32)
    act = (gate * jax.nn.sigmoid(gate)) * up
    acc_ref[...] += jnp.dot(act.astype(BF16), wd_ref[...], preferred_element_type=F32)

    @pl.when(j == n_chunks - 1)
    def _():
        y = DEEPNORM_ALPHA * x_ref[...] + 0.5 * acc_ref[...]
        o = _layer_norm(y, g_ref[...], b_ref[...])
        o_ref[...] = o
        ob_ref[...] = o.astype(BF16)


def ffn_ln(x, w_gu_bf, w_down_bf, g, b, *, tm, tf=512):
    m, d = x.shape
    f = w_down_bf.shape[0]
    n_chunks = f // tf
    assert m % tm == 0 and f % tf == 0
    return pl.pallas_call(
        functools.partial(_ffn_ln_kernel, n_chunks=n_chunks),
        grid=(m // tm, n_chunks),
        in_specs=[
            pl.BlockSpec((tm, d), lambda i, j: (i, 0)),
            pl.BlockSpec((d, tf), lambda i, j: (0, j)),
            pl.BlockSpec((d, tf), lambda i, j: (0, j + n_chunks)),
            pl.BlockSpec((tf, d), lambda i, j: (j, 0)),
            pl.BlockSpec((1, d), lambda i, j: (0, 0)),
            pl.BlockSpec((1, d), lambda i, j: (0, 0)),
        ],
        out_specs=[
            pl.BlockSpec((tm, d), lambda i, j: (i, 0)),
            pl.BlockSpec((tm, d), lambda i, j: (i, 0)),
        ],
        out_shape=[jax.ShapeDtypeStruct((m, d), F32), jax.ShapeDtypeStruct((m, d), BF16)],
        scratch_shapes=[pltpu.VMEM((tm, d), BF16), pltpu.VMEM((tm, d), F32)],
        compiler_params=_cparams("parallel", "arbitrary"),
        name="ffn_ln",
    )(x, w_gu_bf, w_gu_bf, w_down_bf, g.reshape(1, d), b.reshape(1, d))


def _matmul_kernel(x_ref, w_ref, o_ref):
    o_ref[...] = jnp.dot(x_ref[...], w_ref[...], preferred_element_type=F32)


def matmul(x_bf, w_bf, *, tm, tn):
    m, k = x_bf.shape
    n = w_bf.shape[1]
    assert m % tm == 0 and n % tn == 0
    return pl.pallas_call(
        _matmul_kernel,
        grid=(m // tm, n // tn),
        in_specs=[pl.BlockSpec((tm, k), lambda i, j: (i, 0)), pl.BlockSpec((k, tn), lambda i, j: (0, j))],
        out_specs=pl.BlockSpec((tm, tn), lambda i, j: (i, j)),
        out_shape=jax.ShapeDtypeStruct((m, n), F32),
        compiler_params=_cparams("parallel", "arbitrary"),
        name="proj_matmul",
    )(x_bf, w_bf)


def _split2(x):
    hi = x.astype(BF16)
    lo = (x - hi.astype(F32)).astype(BF16)
    return hi, lo


def _split3(x):
    hi = x.astype(BF16)
    r1 = x - hi.astype(F32)
    mid = r1.astype(BF16)
    lo = (r1 - mid.astype(F32)).astype(BF16)
    return hi, mid, lo


def _head_sum(x, bo):
    outs = []
    for c in range(x.shape[1] // LANES):
        hi, mid, lo = _split3(x[:, c * LANES:(c + 1) * LANES])
        s = jnp.dot(hi, bo, preferred_element_type=F32)
        s += jnp.dot(mid, bo, preferred_element_type=F32)
        s += jnp.dot(lo, bo, preferred_element_type=F32)
        outs.append(s)
    return jnp.concatenate(outs, axis=1)


def _rwkv_prep_kernel(ua_ref, first_ref, mu_ref, w0_ref, a0_ref, kk_ref, ka_ref, w2_ref, a2_ref, g2_ref, bo_ref,
                      r_out, lw_out, kx_out, v_out, kkn_out, b_out, g_out, *, period):
    tm = ua_ref.shape[0]
    W = RWKV_WIDTH
    uf = ua_ref[...]
    rolled = pltpu.roll(uf, 1, axis=0)
    row = lax.broadcasted_iota(jnp.int32, (tm, 1), 0)
    if period >= tm:
        first = jnp.broadcast_to(first_ref[0:1, :], uf.shape)
        is_first = row == 0
    else:
        first = first_ref[...]
        is_first = (row % period) == 0
    prev = jnp.where(is_first, first, rolled)
    um = uf + (prev - uf) * mu_ref[...]
    r = um[:, 0:W]
    k = um[:, W:2 * W]
    v = um[:, 2 * W:3 * W]
    tail = um[:, 3 * W:UA_PAD]
    lw = jnp.dot(jnp.tanh(tail).astype(BF16), w2_ref[...], preferred_element_type=F32)
    z = -(w0_ref[...] + lw)
    softplus = jnp.maximum(z, 0.0) + jnp.log1p(jnp.exp(-jnp.abs(z)))
    w_log = -softplus - 0.5
    log_decay = -jnp.exp(w_log)
    a = jax.nn.sigmoid(a0_ref[...] + jnp.dot(tail.astype(BF16), a2_ref[...], preferred_element_type=F32))
    g = jnp.dot(jax.nn.sigmoid(tail).astype(BF16), g2_ref[...], preferred_element_type=F32)
    kk = k * kk_ref[...]
    n2 = _head_sum(kk * kk, bo_ref[...])
    kk = kk / jnp.maximum(jnp.sqrt(n2), 1e-12)
    kx = k * (1.0 + (a - 1.0) * ka_ref[...])
    r_out[...] = r
    lw_out[...] = log_decay
    kx_out[...] = kx
    v_out[...] = v
    kkn_out[...] = kk
    b_out[...] = kk * a
    g_out[...] = g


def rwkv_prep(u, first, consts, *, m0, m, tm, period):
    mu, w0, a0, k_k, k_a, w2p, a2p, g2p, bo = consts
    W = RWKV_WIDTH
    nt = m // tm
    b0 = m0 // tm
    assert m % tm == 0 and m0 % tm == 0
    first_rows = first.shape[0] // nt
    row_spec = lambda width: pl.BlockSpec((1, width), lambda i: (0, 0))
    full = lambda a: pl.BlockSpec(a.shape, lambda i: (0, 0))
    out_tile = pl.BlockSpec((tm, W), lambda i: (i, 0))
    return pl.pallas_call(
        functools.partial(_rwkv_prep_kernel, period=period),
        grid=(nt,),
        in_specs=[
            pl.BlockSpec((tm, UA_PAD), lambda i: (i + b0, OFF_UA // UA_PAD)),
            pl.BlockSpec((first_rows, UA_PAD), lambda i: (i, 0)),
            row_spec(UA_PAD), row_spec(W), row_spec(W), row_spec(W), row_spec(W),
            full(w2p), full(a2p), full(g2p), full(bo),
        ],
        out_specs=[out_tile] * 7,
        out_shape=[jax.ShapeDtypeStruct((m, W), F32)] * 7,
        compiler_params=_cparams("parallel"),
        name="rwkv_prep",
    )(u, first, mu, w0, a0, k_k, k_a, w2p, a2p, g2p, bo)


RWKV_CHUNK = 64
INV_BASE = 16
TN_DIMS = (((0,), (0,)), ((), ()))


def _mm3(a, b, dims=None):
    ah, al = _split2(a)
    bh, bl = _split2(b)
    if dims is None:
        dot = lambda x, y: jnp.dot(x, y, preferred_element_type=F32)
    else:
        dot = lambda x, y: lax.dot_general(x, y, dims, preferred_element_type=F32)
    return dot(ah, bh) + dot(ah, bl) + dot(al, bh)


def _mm_exact_lhs(a_bf, b):
    hi, mid, lo = _split3(b)
    out = jnp.dot(a_bf, hi, preferred_element_type=F32)
    out += jnp.dot(a_bf, mid, preferred_element_type=F32)
    out += jnp.dot(a_bf, lo, preferred_element_type=F32)
    return out


def _rwkv_chunk_kernel(r_ref, lw_ref, kx_ref, kk_ref, b_ref, v_ref, s0_ref, y_ref, sout_ref, s_scr, *, n_tb):
    tb = pl.program_id(1)
    C = RWKV_CHUNK
    R2 = 2 * C

    @pl.when(tb == 0)
    def _():
        s_scr[...] = s0_ref[0]

    ri = lax.broadcasted_iota(jnp.int32, (R2, R2), 0)
    ci = lax.broadcasted_iota(jnp.int32, (R2, R2), 1)
    same_head = (ri // C) == (ci // C)
    lower_strict = same_head & ((ri % C) > (ci % C))
    lower_incl = same_head & ((ri % C) >= (ci % C))
    eye = ri == ci
    same_base = (ri // INV_BASE) == (ci // INV_BASE)
    tri = jnp.where((lax.broadcasted_iota(jnp.int32, (C, C), 0) >= lax.broadcasted_iota(jnp.int32, (C, C), 1)),
                    1.0, 0.0).astype(BF16)
    lane = lax.broadcasted_iota(jnp.int32, (C, LANES), 1)
    left = lane < HALF
    lane_s = lax.broadcasted_iota(jnp.int32, (HALF, LANES), 1)
    left_s = lane_s < HALF

    def stack(x):
        return jnp.concatenate([jnp.where(left, x, 0.0), jnp.where(left, 0.0, x)], axis=0)

    def tile2(x):
        return jnp.concatenate([x, x], axis=0)

    slabs = range(RWKV_HEADS // 2)
    cols = [slice(p * LANES, (p + 1) * LANES) for p in slabs]
    each = lambda f, *xs: [f(*a) for a in zip(*xs)]
    r = [r_ref[0, :, cs] for cs in cols]
    lw = [lw_ref[0, :, cs] for cs in cols]
    kx = [kx_ref[0, :, cs] for cs in cols]
    kk = [kk_ref[0, :, cs] for cs in cols]
    b = [b_ref[0, :, cs] for cs in cols]
    v = [v_ref[0, :, cs] for cs in cols]
    cl = each(lambda x: _mm_exact_lhs(tri, x), lw)
    cl_end = each(lambda x: x[C - 1:C], cl)
    g_inv = each(lambda x: jnp.exp(-x), cl)
    g_end = each(lambda e, x: jnp.exp(e - x), cl_end, cl)
    kt = each(lambda k_, c_, l_: k_ * jnp.exp(c_ - l_), kk, cl, lw)
    rt = each(lambda r_, c_: r_ * jnp.exp(c_), r, cl)
    bh = each(jnp.multiply, b, g_inv)
    kh = each(jnp.multiply, kx, g_inv)
    bbar = each(jnp.multiply, b, g_end)
    kbar = each(jnp.multiply, kx, g_end)
    gram = each(lambda kt_, rt_, bh_, kh_: _mm3(jnp.concatenate([kt_, rt_], axis=0),
                                                jnp.concatenate([stack(bh_), stack(kh_)], axis=0), NT_DIMS),
                kt, rt, bh, kh)
    n_mat = each(lambda g_: jnp.where(lower_strict, tile2(g_[0:C, 0:R2]), 0.0), gram)
    ak = each(lambda g_: jnp.where(lower_strict, tile2(g_[0:C, R2:2 * R2]), 0.0), gram)
    bb = each(lambda g_: jnp.where(lower_incl, tile2(g_[C:R2, 0:R2]), 0.0), gram)
    bk = each(lambda g_: jnp.where(lower_incl, tile2(g_[C:R2, R2:2 * R2]), 0.0), gram)
    d1 = each(lambda n_: jnp.where(same_base, n_, 0.0), n_mat)
    d2 = each(_mm3, d1, d1)
    d4 = each(_mm3, d2, d2)
    d8 = each(_mm3, d4, d4)
    t_inv = each(lambda d_: jnp.where(eye, 1.0, 0.0) - d_, d1)
    for dk in (d2, d4, d8):
        t_inv = each(lambda t_, d_: t_ + _mm3(t_, d_), t_inv, dk)
    blk = INV_BASE
    while blk < C:
        off_mask = ((ri // (2 * blk)) == (ci // (2 * blk))) & ((ri // blk) != (ci // blk))
        tmp = each(lambda t_, n_: _mm3(t_, jnp.where(off_mask, n_, 0.0)), t_inv, n_mat)
        t_inv = each(lambda t_, m_: t_ - _mm3(m_, t_), t_inv, tmp)
        blk *= 2
    v_st = each(stack, v)
    kt_st = each(stack, kt)
    w_st = each(_mm3, ak, v_st)
    pq = each(lambda t_, k_, w_: _mm3(t_, jnp.concatenate([k_, w_], axis=1)), t_inv, kt_st, w_st)
    z = each(lambda bb_, bk_, pq_, v_: _mm3(
        jnp.concatenate([-bb_, bk_], axis=1),
        jnp.concatenate([pq_, jnp.concatenate([jnp.zeros_like(v_), v_], axis=1)], axis=0)), bb, bk, pq, v_st)
    ry_st = each(lambda rt_, z_: stack(rt_) + z_[:, 0:LANES], rt, z)
    bbar_st = each(stack, bbar)
    g_mat = each(lambda e_, pq_, bs_: jnp.where(eye, jnp.broadcast_to(jnp.exp(e_), (R2, LANES)), 0.0)
                 - _mm3(pq_[:, 0:LANES].T, bs_), cl_end, pq, bbar_st)
    h_bd = each(lambda v_, pq_, kb_, bs_: _mm3(jnp.concatenate([v_, pq_[:, LANES:2 * LANES]], axis=0).T,
                                               jnp.concatenate([stack(kb_), -bs_], axis=0)), v_st, pq, kbar, bbar_st)
    for p in slabs:
        s_old = s_scr[p]
        s_dup = jnp.concatenate([jnp.where(left_s, s_old, 0.0), jnp.where(left_s, 0.0, s_old)], axis=0)
        y_st = _mm3(ry_st[p], s_dup, NT_DIMS) + z[p][:, LANES:2 * LANES]
        y_ref[0, :, cols[p]] = y_st[0:C] + y_st[C:R2]
        s_scr[p] = _mm3(s_old, g_mat[p]) + (h_bd[p][0:HALF] + h_bd[p][HALF:LANES])

    @pl.when(tb == n_tb - 1)
    def _():
        sout_ref[0] = s_scr[...]


def rwkv_chunk_scan(r, lw, kx, kk, b, v, s0_slabs):
    bsz, t, W = r.shape
    n_tb = t // RWKV_CHUNK
    assert t % RWKV_CHUNK == 0
    seq = pl.BlockSpec((1, RWKV_CHUNK, W), lambda bi, ti: (bi, ti, 0))
    st = pl.BlockSpec((1, RWKV_HEADS // 2, 64, LANES), lambda bi, ti: (bi, 0, 0, 0))
    return pl.pallas_call(
        functools.partial(_rwkv_chunk_kernel, n_tb=n_tb),
        grid=(bsz, n_tb),
        in_specs=[seq] * 6 + [st],
        out_specs=[seq, st],
        out_shape=[jax.ShapeDtypeStruct((bsz, t, W), F32),
                   jax.ShapeDtypeStruct((bsz, RWKV_HEADS // 2, 64, LANES), F32)],
        scratch_shapes=[pltpu.VMEM((RWKV_HEADS // 2, 64, LANES), F32)],
        compiler_params=_cparams("parallel", "arbitrary"),
        name="rwkv_chunk_scan",
    )(r, lw, kx, kk, b, v, s0_slabs)


def _rwkv_post_kernel(y_ref, r_ref, kx_ref, v_ref, g_ref, rk_ref, lnw_ref, lnb_ref, bo_ref, o_ref):
    bo = bo_ref[...]
    y = y_ref[...]
    inv_n = 1.0 / RWKV_HEAD_SIZE
    ym = _head_sum(y, bo) * inv_n
    d = y - ym
    yv = _head_sum(d * d, bo) * inv_n
    yn = d * lax.rsqrt(yv + RWKV_GN_EPS) * lnw_ref[...] + lnb_ref[...]
    bonus = _head_sum(r_ref[...] * kx_ref[...] * rk_ref[...], bo) * v_ref[...]
    o_ref[...] = ((yn + bonus) * g_ref[...]).astype(BF16)


def rwkv_post(y, r, kx, v, g, r_k, ln_w, ln_b, bo, *, tm):
    m, W = y.shape
    tile = pl.BlockSpec((tm, W), lambda i: (i, 0))
    row = pl.BlockSpec((1, W), lambda i: (0, 0))
    return pl.pallas_call(
        _rwkv_post_kernel,
        grid=(m // tm,),
        in_specs=[tile] * 5 + [row] * 3 + [pl.BlockSpec(bo.shape, lambda i: (0, 0))],
        out_specs=tile,
        out_shape=jax.ShapeDtypeStruct((m, W), BF16),
        compiler_params=_cparams("parallel"),
        name="rwkv_post",
    )(y, r, kx, v, g, r_k, ln_w, ln_b, bo)


def rwkv_consts(mu, w0, w2, a0, a2, g2, k_k, k_a):
    W = RWKV_WIDTH
    tail = UA_PAD - 3 * W
    mu_p = jnp.zeros((1, UA_PAD), F32).at[0, :RWKV_COLS].set(mu.astype(F32))
    o_a = DECAY_LORA
    o_g = DECAY_LORA + AAA_LORA
    w2p = jnp.zeros((tail, W), BF16).at[0:o_a].set(w2.astype(BF16))
    a2p = jnp.zeros((tail, W), BF16).at[o_a:o_g].set(a2.astype(BF16))
    g2p = jnp.zeros((tail, W), BF16).at[o_g:o_g + GATE_LORA].set(g2.astype(BF16))
    half = jnp.arange(LANES) // RWKV_HEAD_SIZE
    bo = (half[:, None] == half[None, :]).astype(BF16)
    row = lambda p: p.astype(F32).reshape(1, W)
    return (mu_p, row(w0), row(a0), row(k_k), row(k_a), w2p, a2p, g2p, bo)


def _state_to_slabs(s):
    bsz = s.shape[0]
    hp = RWKV_HEADS // 2
    return s.reshape(bsz, hp, 2, 64, 64).transpose(0, 1, 3, 2, 4).reshape(bsz, hp, 64, LANES)


def _slabs_to_state(s):
    bsz = s.shape[0]
    hp = RWKV_HEADS // 2
    return s.reshape(bsz, hp, 64, 2, 64).transpose(0, 1, 3, 2, 4).reshape(bsz, RWKV_HEADS, 64, 64)


PREP_TM = 256


def rwkv_time_mix(u, m0, m, bsz, t, shift_state, s0, consts, r_k, ln_w, ln_b, *, prompt):
    W = RWKV_WIDTH
    bo = consts[-1]
    if prompt:
        tm = min(PREP_TM, m)
        nt = m // tm
        prev_rows = u[m0 + tm - 1:m0 + m - 1:tm, OFF_UA:OFF_UA + UA_PAD]
        rows = jnp.concatenate([jnp.zeros((1, UA_PAD), F32), prev_rows], axis=0)
        first = jnp.zeros((nt, SUBLANES, UA_PAD), F32).at[:, 0].set(rows).reshape(nt * SUBLANES, UA_PAD)
        period = tm
        s0_slabs = jnp.zeros((bsz, RWKV_HEADS // 2, 64, LANES), F32)
    else:
        tm = m
        sp = jnp.zeros((bsz, UA_PAD), F32).at[:, :RWKV_COLS].set(shift_state.astype(F32))
        first = jnp.repeat(sp, t, axis=0)
        period = t
        s0_slabs = _state_to_slabs(s0.astype(F32))
    r, lw, kx, v, kk, b, g = rwkv_prep(u, first, consts, m0=m0, m=m, tm=tm, period=period)
    tpad = -(-t // RWKV_CHUNK) * RWKV_CHUNK
    seq = lambda a: jnp.pad(a.reshape(bsz, t, W), ((0, 0), (0, tpad - t), (0, 0)))
    y3, s_fin = rwkv_chunk_scan(seq(r), seq(lw), seq(kx), seq(kk), seq(b), seq(v), s0_slabs)
    y = y3[:, :t].reshape(m, W)
    za = rwkv_post(y, r, kx, v, g, r_k.astype(F32).reshape(1, W), ln_w.astype(F32).reshape(1, W),
                   ln_b.astype(F32).reshape(1, W), bo, tm=tm)
    return za, _slabs_to_state(s_fin)


KV_ROW = 2 * NSA_KV_HEADS * HEAD_DIM
CHUNKS_PER_PAGE = PAGE_SIZE // CMP_STRIDE
PROJ_W = CMP_RANK * KV_ROW


PAGES_PER_STEP = 16


def _page_specs(block, n_pages):
    def spec(i):
        return pl.BlockSpec(block, lambda b, p, pt: (pt[b * n_pages + p * PAGES_PER_STEP + i],) + (0,) * (len(block) - 1))
    return [spec(i) for i in range(PAGES_PER_STEP)]


PAGE_BLOCK = (1, PAGE_SIZE, KV_ROW)


def _cmp_proj_kernel(pt_ref, *refs):
    del pt_ref
    x_refs = refs[:PAGES_PER_STEP]
    w_ref, f_ref, s_ref = refs[PAGES_PER_STEP:]
    half = KV_ROW // 2
    n_out = CMP_RANK * CHUNKS_PER_PAGE
    for i, x_ref in enumerate(x_refs):
        rows = slice(i * CHUNKS_PER_PAGE, (i + 1) * CHUNKS_PER_PAGE)
        for s in range(2):
            out = jnp.dot(w_ref[s], x_ref[0, :, s * half:(s + 1) * half].astype(BF16), preferred_element_type=F32)
            for r in range(CMP_RANK):
                cols = slice(r * KV_ROW + s * half, r * KV_ROW + (s + 1) * half)
                f_ref[0, rows, cols] = out[r * CHUNKS_PER_PAGE:(r + 1) * CHUNKS_PER_PAGE]
                s_ref[0, rows, cols] = out[n_out + r * CHUNKS_PER_PAGE:n_out + (r + 1) * CHUNKS_PER_PAGE]


def cmp_project(pages, page_table_flat, bsz, n_pages, w_proj):
    assert n_pages % PAGES_PER_STEP == 0
    n_ch = n_pages * CHUNKS_PER_PAGE
    step_ch = PAGES_PER_STEP * CHUNKS_PER_PAGE
    out = jax.ShapeDtypeStruct((bsz, n_ch, PROJ_W), F32)
    o_spec = pl.BlockSpec((1, step_ch, PROJ_W), lambda b, p, pt: (b, p, 0))
    w_spec = pl.BlockSpec(w_proj.shape, lambda b, p, pt: (0, 0, 0))
    return pl.pallas_call(
        _cmp_proj_kernel,
        grid_spec=pltpu.PrefetchScalarGridSpec(
            num_scalar_prefetch=1,
            grid=(bsz, n_pages // PAGES_PER_STEP),
            in_specs=_page_specs(PAGE_BLOCK, n_pages) + [w_spec],
            out_specs=[o_spec, o_spec],
        ),
        out_shape=[out, out],
        compiler_params=_cparams("parallel", "arbitrary"),
        name="cmp_project",
    )(page_table_flat, *([pages] * PAGES_PER_STEP), w_proj)


def _cmp_mix_kernel(f_ref, s_ref, waf_ref, was_ref, pex_ref, wb_ref, o_ref):
    n_ch = f_ref.shape[1]
    acc = jnp.zeros((n_ch, KV_ROW), F32)
    for r in range(CMP_RANK):
        rs = slice(r * CMP_STRIDE, (r + 1) * CMP_STRIDE)
        pe = jnp.sum(waf_ref[rs, :] * pex_ref[0:CMP_STRIDE, :], axis=0, keepdims=True)
        pe += jnp.sum(was_ref[rs, :] * pex_ref[CMP_STRIDE:CMP_BLOCK, :], axis=0, keepdims=True)
        cs = slice(r * KV_ROW, (r + 1) * KV_ROW)
        nxt = pltpu.roll(s_ref[0, :, cs], n_ch - 1, axis=0)
        hid = jax.nn.gelu(f_ref[0, :, cs] + nxt + pe)
        acc += jnp.dot(hid.astype(BF16), wb_ref[r], preferred_element_type=F32)
    o_ref[0] = acc.astype(BF16)


def cmp_mix(f, s, waf, was, pex, wb_bd):
    bsz, n_ch, _ = f.shape
    io = pl.BlockSpec((1, n_ch, PROJ_W), lambda b: (b, 0, 0))
    full2 = lambda a: pl.BlockSpec(a.shape, lambda b: (0, 0))
    return pl.pallas_call(
        _cmp_mix_kernel,
        grid=(bsz,),
        in_specs=[io, io, full2(waf), full2(was), full2(pex), pl.BlockSpec(wb_bd.shape, lambda b: (0, 0, 0))],
        out_specs=pl.BlockSpec((1, n_ch, KV_ROW), lambda b: (b, 0, 0)),
        out_shape=jax.ShapeDtypeStruct((bsz, n_ch, KV_ROW), BF16),
        compiler_params=_cparams("parallel"),
        name="cmp_mix",
    )(f, s, waf, was, pex, wb_bd)


def cmp_consts(cmp_pe, cmp_wa, cmp_wb):
    hd = NSA_KV_HEADS * HEAD_DIM
    wa_cols = jnp.repeat(cmp_wa.astype(F32).transpose(2, 1, 0), hd, axis=2)
    waf = wa_cols[:, :CMP_STRIDE].reshape(CMP_RANK * CMP_STRIDE, KV_ROW)
    was = wa_cols[:, CMP_STRIDE:].reshape(CMP_RANK * CMP_STRIDE, KV_ROW)
    pex = jnp.broadcast_to(cmp_pe.astype(F32).transpose(1, 0, 2)[:, :, None, :],
                           (CMP_BLOCK, 2, NSA_KV_HEADS, HEAD_DIM)).reshape(CMP_BLOCK, KV_ROW)
    eye = jnp.eye(NSA_KV_HEADS, dtype=F32)
    wb = cmp_wb.astype(F32)
    bd = jnp.einsum('srde,hg,st->rshdtge', wb, eye, jnp.eye(2, dtype=F32)).reshape(CMP_RANK, KV_ROW, KV_ROW)
    wa4 = cmp_wa.astype(F32).reshape(2, 2, CMP_STRIDE, CMP_RANK)
    w_proj = jnp.einsum('sflr,nm->sfrnml', wa4, jnp.eye(CHUNKS_PER_PAGE, dtype=F32)).reshape(
        2, 2 * CMP_RANK * CHUNKS_PER_PAGE, PAGE_SIZE)
    return waf, was, pex, bd.astype(BF16), w_proj.astype(BF16)


def compress(pages, page_table_flat, bsz, n_pages, consts):
    waf, was, pex, wb_bd, w_proj = consts
    f, s = cmp_project(pages, page_table_flat, bsz, n_pages, w_proj)
    kvc = cmp_mix(f, s, waf, was, pex, wb_bd)
    ratio = SEL_BLOCK // CMP_STRIDE
    n_ch = LANES * ratio
    kvc = jnp.pad(kvc, ((0, 0), (0, n_ch - kvc.shape[1]), (0, 0)))
    return kvc.reshape(bsz, LANES, ratio, KV_ROW).transpose(0, 2, 1, 3).reshape(bsz, n_ch, KV_ROW)


NT_DIMS = (((1,), (1,)), ((), ()))
HALF = LANES // 2
KEY_TILE = 256
WIN_SPAN = WINDOW + Q_BLOCK
TOPK_UNROLL = 8


def _masked_softmax_rows(s, ok):
    x = jnp.where(ok, s, NEG_INF)
    m = jnp.max(x, axis=-1, keepdims=True)
    e = jnp.exp(x - m)
    l = jnp.sum(e, axis=-1, keepdims=True)
    return jnp.where(ok, e / l, 0.0)


def _top_blocks(score, n_keep):
    lane = lax.broadcasted_iota(jnp.int32, score.shape, 1)

    def body(d, cnt):
        rolled = pltpu.roll(score, d, axis=1)
        beats = (rolled > score) | ((rolled == score) & (lane >= d))
        return cnt + jnp.where(beats, 1.0, 0.0)

    cnt = lax.fori_loop(1, LANES, body, jnp.zeros(score.shape, F32), unroll=TOPK_UNROLL)
    return cnt < n_keep


def _rank_unselected(st_ref, n_keep):
    n = st_ref.shape[1]
    sub = lax.broadcasted_iota(jnp.int32, (SUBLANES, LANES), 0)
    cols = []
    for c in range(n // LANES):
        cs = slice(c * LANES, (c + 1) * LANES)
        tiles = [st_ref[t * SUBLANES:(t + 1) * SUBLANES, cs] for t in range(LANES // SUBLANES)]
        cnts = [jnp.zeros((SUBLANES, LANES), F32) for _ in tiles]
        for j in range(LANES):
            sj = st_ref[j:j + 1, cs]
            for t, si in enumerate(tiles):
                if t * SUBLANES > j:
                    beats = sj >= si
                elif (t + 1) * SUBLANES <= j:
                    beats = sj > si
                else:
                    beats = (sj > si) | ((sj == si) & (sub > j - t * SUBLANES))
                cnts[t] = cnts[t] + jnp.where(beats, 1.0, 0.0)
        cols.append(jnp.concatenate(cnts, axis=0))
    return jnp.where(jnp.concatenate(cols, axis=1) < n_keep, 0.0, 1.0).astype(BF16)


def _col_softmax(x):
    m = jnp.max(x, axis=0, keepdims=True)
    e = jnp.exp(x - m)
    l = jnp.sum(e, axis=0, keepdims=True)
    return e * (1.0 / l), m, l


def _nsa_prompt_kernel(q_ref, gn_ref, kc_ref, vct_ref, ks_ref, vst_ref, kw_ref, vwt_ref, o_ref, st_scr):
    i = pl.program_id(0)
    start = i * Q_BLOCK
    assert math.log2(ATTN_SCALE).is_integer()
    q = q_ref[...] * ATTN_SCALE
    gates_t = jax.nn.sigmoid(gn_ref[...]).T
    pos = start + lax.broadcasted_iota(jnp.int32, (1, Q_BLOCK), 1)
    n_cmp = kc_ref.shape[0]
    ratio = SEL_BLOCK // CMP_STRIDE
    crow = lax.broadcasted_iota(jnp.int32, (n_cmp, 1), 0)
    cmp_idx = (crow % LANES) * ratio + crow // LANES
    cmp_ok = (cmp_idx * CMP_STRIDE + (CMP_BLOCK - 1)) <= pos
    cmp_bias = jnp.where(cmp_ok, 0.0, NEG_INF)
    blk = lax.broadcasted_iota(jnp.int32, (LANES, 1), 0)
    cur = pos // SEL_BLOCK
    forced = (blk == 0) | (blk == cur) | (blk == cur - 1)
    future = blk * SEL_BLOCK > pos
    key_sub = lax.broadcasted_iota(jnp.int32, (KEY_TILE, 1), 0)
    blk_lane = lax.broadcasted_iota(jnp.int32, (1, LANES), 1)
    lane_q = lax.broadcasted_iota(jnp.int32, (Q_BLOCK, LANES), 1)
    n_tiles = start // KEY_TILE + 1
    wb0 = jnp.maximum(start - WINDOW, 0) // Q_BLOCK
    n_wblk = WIN_SPAN // Q_BLOCK

    heads = range(NSA_KV_HEADS)
    groups = range(NSA_GROUP)
    ksls = [slice((h // 2) * LANES, (h // 2 + 1) * LANES) for h in heads]
    keeps = [(lane_q // HALF) == (h % 2) for h in heads]
    qs = []
    for h in heads:
        row = []
        for g in groups:
            head = h * NSA_GROUP + g
            slab = q[:, (head // 2) * LANES:(head // 2 + 1) * LANES]
            if head % 2 != h % 2:
                slab = pltpu.roll(slab, HALF, axis=1)
            row.append(jnp.where(keeps[h], slab, 0.0).astype(BF16))
        qs.append(row)

    pairs = [(h, g) for h in heads for g in groups]
    cs_ = [lax.dot_general(kc_ref[:, ksls[h]], qs[h][g], NT_DIMS, preferred_element_type=F32) + cmp_bias
           for h, g in pairs]
    pcs = [jnp.where(cmp_ok, _col_softmax(x)[0], 0.0) for x in cs_]
    o_c = [jnp.dot(vct_ref[ksls[h], :], pcs[i].astype(BF16), preferred_element_type=F32)
           for i, (h, g) in enumerate(pairs)]
    for h in heads:
        imp = pcs[h * NSA_GROUP]
        for g in range(1, NSA_GROUP):
            imp = imp + pcs[h * NSA_GROUP + g]
        imp_blk = imp[0:LANES]
        for j in range(1, ratio):
            imp_blk = imp_blk + imp[j * LANES:(j + 1) * LANES]
        st_scr[:, h * Q_BLOCK:(h + 1) * Q_BLOCK] = jnp.where(future, -FORCE, jnp.where(forced, FORCE, imp_blk))
    unsel = _rank_unselected(st_scr, N_SEL)

    def sel_tile(kt, carry, causal):
        k0 = kt * KEY_TILE
        expand = jnp.where((k0 + key_sub) // SEL_BLOCK == blk_lane, NEG_INF, 0.0).astype(BF16)
        biases = [jnp.dot(expand, unsel[:, h * Q_BLOCK:(h + 1) * Q_BLOCK], preferred_element_type=F32) for h in heads]
        if causal:
            visible = (k0 + key_sub) <= pos
            biases = [jnp.where(visible, b_, NEG_INF) for b_ in biases]
        pairs = [(h, g) for h in heads for g in groups]
        xs = [lax.dot_general(ks_ref[kt, :, ksls[h]], qs[h][g], NT_DIMS, preferred_element_type=F32) + biases[h]
              for h, g in pairs]
        m_new = [jnp.maximum(carry[i][0], jnp.max(x, axis=0, keepdims=True)) for i, x in enumerate(xs)]
        ps = [jnp.exp(x - m) for x, m in zip(xs, m_new)]
        scs = [jnp.exp(carry[i][0] - m) for i, m in enumerate(m_new)]
        l_new = [sc * carry[i][1] + jnp.sum(p, axis=0, keepdims=True) for i, (sc, p) in enumerate(zip(scs, ps))]
        pvs = [jnp.dot(vst_ref[kt, ksls[h], :], ps[i].astype(BF16), preferred_element_type=F32)
               for i, (h, g) in enumerate(pairs)]
        return tuple((m_new[i], l_new[i], scs[i] * carry[i][2] + pvs[i]) for i in range(len(pairs)))

    init = tuple((jnp.full((1, Q_BLOCK), NEG_INF, F32), jnp.zeros((1, Q_BLOCK), F32), jnp.zeros((LANES, Q_BLOCK), F32))
                 for _ in range(NSA_HEADS))
    carry = lax.fori_loop(0, n_tiles - 1, functools.partial(sel_tile, causal=False), init)
    carry = sel_tile(n_tiles - 1, carry, True)

    kpos = wb0 * Q_BLOCK + lax.broadcasted_iota(jnp.int32, (WIN_SPAN, 1), 0)
    delta = pos - kpos
    win_bias = jnp.where((delta >= 0) & (delta < WINDOW), 0.0, NEG_INF)
    k_ws = [jnp.concatenate([kw_ref[wb0 + j, :, ksls[h]] for j in range(n_wblk)], axis=0) for h in heads]
    ws_ = [lax.dot_general(k_ws[h], qs[h][g], NT_DIMS, preferred_element_type=F32) + win_bias for h, g in pairs]
    pws = [_col_softmax(x)[0].astype(BF16) for x in ws_]
    o_w = []
    for i, (h, g) in enumerate(pairs):
        acc = jnp.dot(vwt_ref[wb0, ksls[h], :], pws[i][0:Q_BLOCK], preferred_element_type=F32)
        for j in range(1, n_wblk):
            acc += jnp.dot(vwt_ref[wb0 + j, ksls[h], :], pws[i][j * Q_BLOCK:(j + 1) * Q_BLOCK],
                           preferred_element_type=F32)
        o_w.append(acc)
    pieces = []
    for head, (h, g) in enumerate(pairs):
        _, l_s, acc_s = carry[head]
        o_s = acc_s * (1.0 / l_s)
        mix = (gates_t[head:head + 1] * o_c[head] + gates_t[NSA_HEADS + head:NSA_HEADS + head + 1] * o_s
               + gates_t[2 * NSA_HEADS + head:2 * NSA_HEADS + head + 1] * o_w[head])
        pieces.append(mix[(h % 2) * HALF:(h % 2 + 1) * HALF])
    o_t = jnp.concatenate(pieces, axis=0)
    o_ref[...] = o_t.T.astype(BF16)


def nsa_prompt(u, t, kvc, sel_kv_bf, win_kv_bf):
    n_qb = t // Q_BLOCK
    hw = NSA_KV_HEADS * HEAD_DIM
    tiles = lambda a, rows: a.reshape(t // rows, rows, hw)
    tiles_t = lambda a, rows: a.reshape(t // rows, rows, hw).transpose(0, 2, 1)
    operands = [
        kvc[:, :hw], kvc[:, hw:].T,
        tiles(sel_kv_bf[:, :hw], KEY_TILE), tiles_t(sel_kv_bf[:, hw:], KEY_TILE),
        tiles(win_kv_bf[:, :hw], Q_BLOCK), tiles_t(win_kv_bf[:, hw:], Q_BLOCK),
    ]
    whole = lambda a: pl.BlockSpec(a.shape, lambda i: (0,) * a.ndim)
    return pl.pallas_call(
        _nsa_prompt_kernel,
        grid=(n_qb,),
        in_specs=[
            pl.BlockSpec((Q_BLOCK, Q_COLS), lambda i: (i, OFF_Q // Q_COLS)),
            pl.BlockSpec((Q_BLOCK, GN_PAD), lambda i: (i, OFF_GN // GN_PAD)),
        ] + [whole(a) for a in operands],
        out_specs=pl.BlockSpec((Q_BLOCK, NSA_WIDTH), lambda i: (i, 0)),
        out_shape=jax.ShapeDtypeStruct((t, NSA_WIDTH), BF16),
        scratch_shapes=[pltpu.VMEM((LANES, NSA_KV_HEADS * Q_BLOCK), F32)],
        compiler_params=_cparams("arbitrary"),
        name="nsa_prompt",
    )(u, u, *operands)


def _tail_keys(qf, k_new, v_new, tok, m, l, acc):
    nt = k_new.shape[0]
    kn = k_new.astype(BF16).astype(F32)
    vn = v_new.astype(BF16).astype(F32)
    s = [jnp.sum(qf * kn[j:j + 1], axis=1, keepdims=True) * ATTN_SCALE for j in range(nt)]
    ok = [tok >= j for j in range(nt)]
    m_new = m
    for j in range(nt):
        m_new = jnp.maximum(m_new, jnp.where(ok[j], s[j], NEG_INF))
    scale = jnp.exp(m - m_new)
    l = scale * l
    acc = scale * acc
    for j in range(nt):
        p = jnp.where(ok[j], jnp.exp(s[j] - m_new), 0.0)
        l = l + p
        acc = acc + p * vn[j:j + 1]
    return m_new, l, acc


def _nsa_sample_kernel(pt_ref, q_ref, gn_ref, kc_ref, vc_ref, *refs, nt, past, n_steps):
    del pt_ref
    page_refs = refs[:PAGES_PER_STEP]
    (snew_ref, wnew_ref, win_ref, o_ref,
     qh_scr, sel_scr, oc_scr, m_scr, l_scr, acc_scr, kb_scr, vb_scr) = refs[PAGES_PER_STEP:]
    p = pl.program_id(1)
    rows = NSA_GROUP * nt
    hw = NSA_KV_HEADS * HEAD_DIM
    row1 = lax.broadcasted_iota(jnp.int32, (rows, 1), 0)
    tok = row1 % nt
    grp = row1 // nt
    pos = past + tok
    lane = lax.broadcasted_iota(jnp.int32, (rows, LANES), 1)
    blk = lax.broadcasted_iota(jnp.int32, (1, LANES), 1)
    ratio = SEL_BLOCK // CMP_STRIDE

    @pl.when(p == 0)
    def _():
        q16 = q_ref[0]
        n_cmp = kc_ref.shape[1]
        col = lax.broadcasted_iota(jnp.int32, (1, n_cmp), 1)
        cmp_idx = (col % LANES) * ratio + col // LANES
        cmp_ok = (cmp_idx * CMP_STRIDE + (CMP_BLOCK - 1)) <= pos
        cur = pos // SEL_BLOCK
        forced = (blk == 0) | (blk == cur) | (blk == cur - 1)
        future = blk * SEL_BLOCK > pos
        for h in range(NSA_KV_HEADS):
            khalf = h % 2
            ksl = slice((h // 2) * LANES, (h // 2 + 1) * LANES)
            keep = (lane // HALF) == khalf
            qh = jnp.zeros((rows, LANES), F32)
            for g in range(NSA_GROUP):
                head = h * NSA_GROUP + g
                slab = q16[:, (head // 2) * LANES:(head // 2 + 1) * LANES]
                if head % 2 != khalf:
                    slab = pltpu.roll(slab, HALF, axis=1)
                qh = jnp.where((grp == g) & keep, slab, qh)
            qh_scr[h] = qh
            sc = lax.dot_general(qh.astype(BF16), kc_ref[0, :, ksl], NT_DIMS, preferred_element_type=F32) * ATTN_SCALE
            pc = _masked_softmax_rows(sc, cmp_ok)
            oc_scr[h] = jnp.dot(pc.astype(BF16), vc_ref[0, :, ksl], preferred_element_type=F32)
            imp = pc
            for g in range(1, NSA_GROUP):
                imp = imp + pltpu.roll(pc, g * nt, axis=0)
            imp_blk = imp[:, 0:LANES]
            for j in range(1, ratio):
                imp_blk = imp_blk + imp[:, j * LANES:(j + 1) * LANES]
            score = jnp.where(future, -FORCE, jnp.where(forced, FORCE, imp_blk))
            sel_scr[h] = jnp.where(_top_blocks(score, N_SEL - 1), 1.0, 0.0)
            m_scr[h] = jnp.full((rows, 1), NEG_INF, F32)
            l_scr[h] = jnp.zeros((rows, 1), F32)
            acc_scr[h] = jnp.zeros((rows, LANES), F32)

    step_keys = PAGES_PER_STEP * PAGE_SIZE
    for i, page_ref in enumerate(page_refs):
        ks = slice(i * PAGE_SIZE, (i + 1) * PAGE_SIZE)
        kb_scr[ks, :] = page_ref[0, :, 0:hw].astype(BF16)
        vb_scr[ks, :] = page_ref[0, :, hw:2 * hw].astype(BF16)
    blk_row = lax.broadcasted_iota(jnp.int32, (LANES, 1), 0)
    key_lane = lax.broadcasted_iota(jnp.int32, (1, step_keys), 1)
    expand = jnp.where(blk_row == (p * step_keys + key_lane) // SEL_BLOCK, 1.0, 0.0).astype(BF16)
    for slab in range(NSA_KV_HEADS // 2):
        h0, h1 = 2 * slab, 2 * slab + 1
        ksl = slice(slab * LANES, (slab + 1) * LANES)
        pair = lambda ref: jnp.concatenate([ref[h0], ref[h1]], axis=0)
        s = lax.dot_general(pair(qh_scr).astype(BF16), kb_scr[:, ksl], NT_DIMS,
                            preferred_element_type=F32) * ATTN_SCALE
        ok = jnp.dot(pair(sel_scr).astype(BF16), expand, preferred_element_type=F32) > 0.5
        x = jnp.where(ok, s, NEG_INF)
        m_i = pair(m_scr)
        m_new = jnp.maximum(m_i, jnp.max(x, axis=-1, keepdims=True))
        pr = jnp.where(ok, jnp.exp(x - m_new), 0.0)
        scale = jnp.exp(m_i - m_new)
        l_new = scale * pair(l_scr) + jnp.sum(pr, axis=-1, keepdims=True)
        acc_new = scale * pair(acc_scr) + jnp.dot(pr.astype(BF16), vb_scr[:, ksl], preferred_element_type=F32)
        for h, rs in ((h0, slice(0, rows)), (h1, slice(rows, 2 * rows))):
            m_scr[h] = m_new[rs]
            l_scr[h] = l_new[rs]
            acc_scr[h] = acc_new[rs]

    @pl.when(p == n_steps - 1)
    def _():
        gates = jax.nn.sigmoid(gn_ref[0])
        snew = snew_ref[0]
        wnew = wnew_ref[0]
        wrows = win_ref.shape[1]
        kwb = win_ref[0, :, 0:hw].astype(BF16)
        vwb = win_ref[0, :, hw:2 * hw].astype(BF16)
        widx = lax.broadcasted_iota(jnp.int32, (1, wrows), 1)
        win_ok = widx > tok + (wrows - WINDOW)
        stacks = []
        for h in range(NSA_KV_HEADS):
            ksl = slice((h // 2) * LANES, (h // 2 + 1) * LANES)
            vsl = slice(hw + (h // 2) * LANES, hw + (h // 2 + 1) * LANES)
            qb = qh_scr[h].astype(BF16)
            qf = qb.astype(F32)
            _, l_s, acc_s = _tail_keys(qf, snew[:, ksl], snew[:, vsl], tok, m_scr[h], l_scr[h], acc_scr[h])
            o_s = acc_s / l_s
            sw = lax.dot_general(qb, kwb[:, ksl], NT_DIMS, preferred_element_type=F32) * ATTN_SCALE
            x = jnp.where(win_ok, sw, NEG_INF)
            m_w = jnp.max(x, axis=-1, keepdims=True)
            pw = jnp.where(win_ok, jnp.exp(x - m_w), 0.0)
            l_w = jnp.sum(pw, axis=-1, keepdims=True)
            acc_w = jnp.dot(pw.astype(BF16), vwb[:, ksl], preferred_element_type=F32)
            _, l_w, acc_w = _tail_keys(qf, wnew[:, ksl], wnew[:, vsl], tok, m_w, l_w, acc_w)
            o_w = acc_w / l_w
            gc = jnp.zeros((rows, 1), F32)
            gs = jnp.zeros((rows, 1), F32)
            gw = jnp.zeros((rows, 1), F32)
            for g in range(NSA_GROUP):
                head = h * NSA_GROUP + g
                gc = jnp.where(grp == g, gates[:, head:head + 1], gc)
                gs = jnp.where(grp == g, gates[:, NSA_HEADS + head:NSA_HEADS + head + 1], gs)
                gw = jnp.where(grp == g, gates[:, 2 * NSA_HEADS + head:2 * NSA_HEADS + head + 1], gw)
            stacks.append(gc * oc_scr[h] + gs * o_s + gw * o_w)
        for sidx in range(NSA_HEADS // 2):
            h = sidx // 2
            halves = []
            for g in (2 * (sidx % 2), 2 * (sidx % 2) + 1):
                part = stacks[h]
                if g > 0:
                    part = pltpu.roll(part, rows - g * nt, axis=0)
                if g % 2 != h % 2:
                    part = pltpu.roll(part, HALF, axis=1)
                halves.append(part)
            slab = jnp.where(lane < HALF, halves[0], halves[1])
            o_ref[0, :, sidx * LANES:(sidx + 1) * LANES] = slab[0:nt]


def nsa_sample(q16, gn16, kvc, sel_pages, page_table_flat, snew, wnew, win_state, *, nt, past):
    bsz, rows, _ = q16.shape
    n_pages = past // PAGE_SIZE
    assert past % SEL_BLOCK == 0 and nt <= SEL_BLOCK and rows == NSA_GROUP * nt
    assert n_pages % PAGES_PER_STEP == 0
    n_steps = n_pages // PAGES_PER_STEP
    step_keys = PAGES_PER_STEP * PAGE_SIZE
    hw = NSA_KV_HEADS * HEAD_DIM
    per_b = lambda shape, cb=0: pl.BlockSpec((1,) + shape, lambda b, p, pt: (b, 0, cb))
    hs = NSA_KV_HEADS
    return pl.pallas_call(
        functools.partial(_nsa_sample_kernel, nt=nt, past=past, n_steps=n_steps),
        grid_spec=pltpu.PrefetchScalarGridSpec(
            num_scalar_prefetch=1,
            grid=(bsz, n_steps),
            in_specs=[
                per_b((rows, Q_COLS)), per_b((rows, GN_PAD)),
                per_b((kvc.shape[1], hw), 0), per_b((kvc.shape[1], hw), 1),
                *_page_specs(PAGE_BLOCK, n_pages),
                per_b((nt, KV_ROW)), per_b((nt, KV_ROW)),
                per_b((win_state.shape[1], KV_ROW)),
            ],
            out_specs=pl.BlockSpec((1, nt, NSA_WIDTH), lambda b, p, pt: (b, 0, 0)),
            scratch_shapes=[
                pltpu.VMEM((hs, rows, LANES), F32), pltpu.VMEM((hs, rows, LANES), F32),
                pltpu.VMEM((hs, rows, LANES), F32), pltpu.VMEM((hs, rows, 1), F32),
                pltpu.VMEM((hs, rows, 1), F32), pltpu.VMEM((hs, rows, LANES), F32),
                pltpu.VMEM((step_keys, hw), BF16), pltpu.VMEM((step_keys, hw), BF16),
            ],
        ),
        out_shape=jax.ShapeDtypeStruct((bsz, nt, NSA_WIDTH), F32),
        compiler_params=_cparams("parallel", "arbitrary"),
        name="nsa_sample",
    )(page_table_flat, q16, gn16, kvc, kvc, *([sel_pages] * PAGES_PER_STEP), snew, wnew, win_state)


def _merge_kernel(za_ref, ob_ref, g0_ref, g1_ref, h_ref, wua_ref, wub_ref, wo_ref, g_ref, b_ref, o_ref, obf_ref):
    ya = jnp.dot(za_ref[...], wua_ref[...], preferred_element_type=F32)
    yb = jnp.dot(ob_ref[...], wub_ref[...], preferred_element_type=F32)
    mixed = jax.nn.sigmoid(g0_ref[...]) * ya + jax.nn.sigmoid(g1_ref[...]) * yb
    mix = jnp.dot(mixed.astype(BF16), wo_ref[...], preferred_element_type=F32)
    o = _layer_norm(DEEPNORM_ALPHA * h_ref[...] + mix, g_ref[...], b_ref[...])
    o_ref[...] = o
    obf_ref[...] = o.astype(BF16)


def merge_ln(za, ob, u, h, wua, wub, wo, g, b, *, tm):
    m, d = h.shape
    W = za.shape[1]
    assert m % tm == 0
    tile = lambda w, cb=0: pl.BlockSpec((tm, w), lambda i: (i, cb))
    const = lambda a: pl.BlockSpec(a.shape, lambda i: (0, 0), pipeline_mode=pl.Buffered(1))
    row = pl.BlockSpec((1, d), lambda i: (0, 0))
    return pl.pallas_call(
        _merge_kernel,
        grid=(m // tm,),
        in_specs=[tile(W), tile(W), tile(d, OFF_GM // d), tile(d, OFF_GM // d + 1), tile(d),
                  const(wua), const(wub), const(wo), row, row],
        out_specs=[tile(d), tile(d)],
        out_shape=[jax.ShapeDtypeStruct((m, d), F32), jax.ShapeDtypeStruct((m, d), BF16)],
        compiler_params=_cparams("parallel"),
        name="merge_ln",
    )(za, ob, u, u, h, wua, wub, wo, g.reshape(1, d), b.reshape(1, d))


TOKEN_TM = 640
PROJ_TN = 2176
MERGE_TM = 320


def kernel(x_prompt, x_sample, cache_cmp_kv, cache_sel_kv, page_table, state_win_kv, state_rwkv, state_rwkv_shift,
           ln1_g, ln1_b, ffn1_w_gu, ffn1_w_down, w_in, rwkv_mu, rwkv_w0, rwkv_w2, rwkv_a0, rwkv_a2, rwkv_g2,
           rwkv_k_k, rwkv_k_a, rwkv_r_k, rwkv_ln_w, rwkv_ln_b, w_up_a, cmp_pe, cmp_wa, cmp_wb, w_up_b, w_o,
           ln2_g, ln2_b, ffn2_w_gu, ffn2_w_down, ln3_g, ln3_b):
    bp, tp, d = x_prompt.shape
    bs, ts, _ = x_sample.shape
    assert bp == 1
    mp, ms = bp * tp, bs * ts
    n_pool = cache_cmp_kv.shape[0]
    n_pages = page_table.shape[1]
    past = n_pages * PAGE_SIZE
    kvh = (2, NSA_KV_HEADS, HEAD_DIM)
    bf = lambda a: a.astype(BF16)
    f32 = lambda a: a.astype(F32)

    x = jnp.concatenate([f32(x_prompt).reshape(mp, d), f32(x_sample).reshape(ms, d)], axis=0)
    h1, h1_bf = ffn_ln(x, bf(ffn1_w_gu), bf(ffn1_w_down), f32(ln1_g), f32(ln1_b), tm=TOKEN_TM)
    u = matmul(h1_bf, bf(pad_in_cols(w_in)), tm=TOKEN_TM, tn=PROJ_TN)
    u_s = u[mp:].reshape(bs, ts, IN_PAD)
    kv_p = u[:mp, OFF_KV:OFF_KV + KV_COLS]
    cmp_p, sel_p, win_p = kv_p[:, 0:KV_ROW], kv_p[:, KV_ROW:2 * KV_ROW], kv_p[:, 2 * KV_ROW:3 * KV_ROW]
    cmp_s = u_s[:, :, OFF_KV:OFF_KV + KV_ROW]
    sel_s = u_s[:, :, OFF_KV + KV_ROW:OFF_KV + 2 * KV_ROW]
    win_s = u_s[:, :, OFF_KV + 2 * KV_ROW:OFF_KV + 3 * KV_ROW]

    rc = rwkv_consts(rwkv_mu, rwkv_w0, rwkv_w2, rwkv_a0, rwkv_a2, rwkv_g2, rwkv_k_k, rwkv_k_a)
    za_p, s_p = rwkv_time_mix(u, 0, mp, bp, tp, None, None, rc, rwkv_r_k, rwkv_ln_w, rwkv_ln_b, prompt=True)
    za_s, s_s = rwkv_time_mix(u, mp, ms, bs, ts, state_rwkv_shift, state_rwkv, rc, rwkv_r_k, rwkv_ln_w, rwkv_ln_b,
                              prompt=False)

    cc = cmp_consts(cmp_pe, cmp_wa, cmp_wb)
    as_pages = lambda a: f32(a).reshape(-1, PAGE_SIZE, KV_ROW)
    kvc_p = compress(as_pages(cmp_p), jnp.arange(mp // PAGE_SIZE, dtype=jnp.int32), 1, mp // PAGE_SIZE, cc)[0]
    o_p = nsa_prompt(u, mp, kvc_p, bf(sel_p), bf(win_p))
    pt_flat = page_table.reshape(-1).astype(jnp.int32)
    kvc_s = compress(as_pages(cache_cmp_kv), pt_flat, bs, n_pages, cc)
    q16 = jnp.tile(u_s[:, :, OFF_Q:OFF_Q + Q_COLS], (1, NSA_GROUP, 1))
    gn16 = jnp.tile(u_s[:, :, OFF_GN:OFF_GN + GN_PAD], (1, NSA_GROUP, 1))
    o_s = nsa_sample(q16, gn16, kvc_s, as_pages(cache_sel_kv), pt_flat, sel_s, win_s,
                     f32(state_win_kv).reshape(bs, -1, KV_ROW), nt=ts, past=past)

    za = jnp.concatenate([za_p, za_s], axis=0)
    ob = jnp.concatenate([o_p, bf(o_s.reshape(ms, NSA_WIDTH))], axis=0)
    h2, _ = merge_ln(za, ob, u, h1, bf(w_up_a), bf(w_up_b), bf(w_o), f32(ln2_g), f32(ln2_b), tm=MERGE_TM)
    y, _ = ffn_ln(h2, bf(ffn2_w_gu), bf(ffn2_w_down), f32(ln3_g), f32(ln3_b), tm=TOKEN_TM)

    wb = min(WINDOW, tp)
    wkeep = min(WINDOW, state_win_kv.shape[1] + ts)
    win_all = jnp.concatenate([state_win_kv, win_s.reshape(bs, ts, *kvh).astype(state_win_kv.dtype)], axis=1)
    cd, sd, wd = cache_cmp_kv.dtype, cache_sel_kv.dtype, state_win_kv.dtype
    rd, hd = state_rwkv.dtype, state_rwkv_shift.dtype
    return (
        y[:mp].reshape(bp, tp, d).astype(x_prompt.dtype),
        y[mp:].reshape(bs, ts, d).astype(x_sample.dtype),
        cmp_p.reshape(bp, tp, *kvh).astype(cd),
        sel_p.reshape(bp, tp, *kvh).astype(sd),
        win_p[tp - wb:].reshape(bp, wb, *kvh).astype(wd),
        s_p.astype(rd),
        u[mp - 1:mp, OFF_UA:OFF_UA + RWKV_COLS].astype(hd),
        cmp_s.reshape(bs, ts, *kvh).astype(cd),
        sel_s.reshape(bs, ts, *kvh).astype(sd),
        win_all[:, win_all.shape[1] - wkeep:].astype(wd),
        s_s.astype(rd),
        u_s[:, ts - 1, OFF_UA:OFF_UA + RWKV_COLS].astype(hd),
    )
```

```python
import functools
import math

import jax
import jax.numpy as jnp
from jax import lax
from jax.experimental import pallas as pl
from jax.experimental.pallas import tpu as pltpu

F32 = jnp.float32
BF16 = jnp.bfloat16

D_MODEL = 2048
RWKV_WIDTH = D_MODEL // 2
RWKV_HEAD_SIZE = 64
RWKV_HEADS = RWKV_WIDTH // RWKV_HEAD_SIZE
DECAY_LORA = 64
AAA_LORA = 64
GATE_LORA = 160
RWKV_GN_EPS = 64e-5
NSA_WIDTH = D_MODEL // 2
HEAD_DIM = 64
NSA_HEADS = NSA_WIDTH // HEAD_DIM
NSA_KV_HEADS = 4
NSA_GROUP = NSA_HEADS // NSA_KV_HEADS
CMP_STRIDE = 16
CMP_BLOCK = 2 * CMP_STRIDE
CMP_RANK = 4
SEL_BLOCK = 64
N_SEL = 16
WINDOW = 512
Q_BLOCK = 128
PAGE_SIZE = 128
ATTN_SCALE = HEAD_DIM ** -0.5
FFN_HIDDEN = 256 * math.ceil(8 * D_MODEL / 3 / 256)
DEPTH = 1
DEEPNORM_ALPHA = (2 * DEPTH) ** 0.25
LN_EPS = 1e-5
NEG_INF = -1e30
FORCE = 1e9
RWKV_COLS = 3 * RWKV_WIDTH + DECAY_LORA + AAA_LORA + GATE_LORA
Q_COLS = NSA_HEADS * HEAD_DIM
KV_COLS = 3 * 2 * NSA_KV_HEADS * HEAD_DIM
NSA_GATE_COLS = 3 * NSA_HEADS
MERGE_GATE_COLS = 2 * D_MODEL

LANES = 128
SUBLANES = 8
VMEM_LIMIT_BYTES = 56 * 1024 * 1024

UA_PAD = 3584
GN_PAD = 128
OFF_GM = 0
OFF_Q = OFF_GM + MERGE_GATE_COLS
OFF_KV = OFF_Q + Q_COLS
OFF_UA = 2 * UA_PAD
OFF_GN = OFF_UA + UA_PAD
IN_PAD = OFF_GN + GN_PAD


def pad_in_cols(w):
    o = 0
    ua = w[..., o:o + RWKV_COLS]; o += RWKV_COLS
    q = w[..., o:o + Q_COLS]; o += Q_COLS
    kv = w[..., o:o + KV_COLS]; o += KV_COLS
    gn = w[..., o:o + NSA_GATE_COLS]; o += NSA_GATE_COLS
    gm = w[..., o:o + MERGE_GATE_COLS]
    z = lambda n: jnp.zeros(w.shape[:-1] + (n,), w.dtype)
    return jnp.concatenate([gm, q, kv, z(OFF_UA - (OFF_KV + KV_COLS)), ua, z(UA_PAD - RWKV_COLS), gn,
                            z(GN_PAD - NSA_GATE_COLS)], axis=-1)


def _cparams(*sem):
    return pltpu.CompilerParams(dimension_semantics=sem, vmem_limit_bytes=VMEM_LIMIT_BYTES)


def _layer_norm(y, g, b):
    mu = jnp.mean(y, axis=-1, keepdims=True)
    d = y - mu
    var = jnp.mean(d * d, axis=-1, keepdims=True)
    return d * lax.rsqrt(var + LN_EPS) * g + b


def _ffn_ln_kernel(x_ref, wg_ref, wu_ref, wd_ref, g_ref, b_ref, o_ref, ob_ref, xb_ref, acc_ref, *, n_chunks):
    j = pl.program_id(1)

    @pl.when(j == 0)
    def _():
        xb_ref[...] = x_ref[...].astype(BF16)
        acc_ref[...] = jnp.zeros_like(acc_ref)

    xb = xb_ref[...]
    gate = jnp.dot(xb, wg_ref[...], preferred_element_type=F32)
    up = jnp.dot(xb, wu_ref[...], preferred_element_type=F32)
    act = (gate * jax.nn.sigmoid(gate)) * up
    acc_ref[...] += jnp.dot(act.astype(BF16), wd_ref[...], preferred_element_type=F32)

    @pl.when(j == n_chunks - 1)
    def _():
        y = DEEPNORM_ALPHA * x_ref[...] + 0.5 * acc_ref[...]
        o = _layer_norm(y, g_ref[...], b_ref[...])
        o_ref[...] = o
        ob_ref[...] = o.astype(BF16)


def ffn_ln(x, w_gu_bf, w_down_bf, g, b, *, tm, tf=512):
    m, d = x.shape
    f = w_down_bf.shape[0]
    n_chunks = f // tf
    assert m % tm == 0 and f % tf == 0
    return pl.pallas_call(
        functools.partial(_ffn_ln_kernel, n_chunks=n_chunks),
        grid=(m // tm, n_chunks),
        in_specs=[
            pl.BlockSpec((tm, d), lambda i, j: (i, 0)),
            pl.BlockSpec((d, tf), lambda i, j: (0, j)),
            pl.BlockSpec((d, tf), lambda i, j: (0, j + n_chunks)),
            pl.BlockSpec((tf, d), lambda i, j: (j, 0)),
            pl.BlockSpec((1, d), lambda i, j: (0, 0)),
            pl.BlockSpec((1, d), lambda i, j: (0, 0)),
        ],
        out_specs=[
            pl.BlockSpec((tm, d), lambda i, j: (i, 0)),
            pl.BlockSpec((tm, d), lambda i, j: (i, 0)),
        ],
        out_shape=[jax.ShapeDtypeStruct((m, d), F32), jax.ShapeDtypeStruct((m, d), BF16)],
        scratch_shapes=[pltpu.VMEM((tm, d), BF16), pltpu.VMEM((tm, d), F32)],
        compiler_params=_cparams("parallel", "arbitrary"),
        name="ffn_ln",
    )(x, w_gu_bf, w_gu_bf, w_down_bf, g.reshape(1, d), b.reshape(1, d))


def _matmul_kernel(x_ref, w_ref, o_ref):
    o_ref[...] = jnp.dot(x_ref[...], w_ref[...], preferred_element_type=F32)


def matmul(x_bf, w_bf, *, tm, tn):
    m, k = x_bf.shape
    n = w_bf.shape[1]
    assert m % tm == 0 and n % tn == 0
    return pl.pallas_call(
        _matmul_kernel,
        grid=(m // tm, n // tn),
        in_specs=[pl.BlockSpec((tm, k), lambda i, j: (i, 0)), pl.BlockSpec((k, tn), lambda i, j: (0, j))],
        out_specs=pl.BlockSpec((tm, tn), lambda i, j: (i, j)),
        out_shape=jax.ShapeDtypeStruct((m, n), F32),
        compiler_params=_cparams("parallel", "arbitrary"),
        name="proj_matmul",
    )(x_bf, w_bf)


def _split2(x):
    hi = x.astype(BF16)
    lo = (x - hi.astype(F32)).astype(BF16)
    return hi, lo


def _split3(x):
    hi = x.astype(BF16)
    r1 = x - hi.astype(F32)
    mid = r1.astype(BF16)
    lo = (r1 - mid.astype(F32)).astype(BF16)
    return hi, mid, lo


def _head_sum(x, bo):
    outs = []
    for c in range(x.shape[1] // LANES):
        hi, mid, lo = _split3(x[:, c * LANES:(c + 1) * LANES])
        s = jnp.dot(hi, bo, preferred_element_type=F32)
        s += jnp.dot(mid, bo, preferred_element_type=F32)
        s += jnp.dot(lo, bo, preferred_element_type=F32)
        outs.append(s)
    return jnp.concatenate(outs, axis=1)


def _rwkv_prep_kernel(ua_ref, first_ref, mu_ref, w0_ref, a0_ref, kk_ref, ka_ref, w2_ref, a2_ref, g2_ref, bo_ref,
                      r_out, lw_out, kx_out, v_out, kkn_out, b_out, g_out, *, period):
    tm = ua_ref.shape[0]
    W = RWKV_WIDTH
    uf = ua_ref[...]
    rolled = pltpu.roll(uf, 1, axis=0)
    row = lax.broadcasted_iota(jnp.int32, (tm, 1), 0)
    if period >= tm:
        first = jnp.broadcast_to(first_ref[0:1, :], uf.shape)
        is_first = row == 0
    else:
        first = first_ref[...]
        is_first = (row % period) == 0
    prev = jnp.where(is_first, first, rolled)
    um = uf + (prev - uf) * mu_ref[...]
    r = um[:, 0:W]
    k = um[:, W:2 * W]
    v = um[:, 2 * W:3 * W]
    tail = um[:, 3 * W:UA_PAD]
    lw = jnp.dot(jnp.tanh(tail).astype(BF16), w2_ref[...], preferred_element_type=F32)
    z = -(w0_ref[...] + lw)
    softplus = jnp.maximum(z, 0.0) + jnp.log1p(jnp.exp(-jnp.abs(z)))
    w_log = -softplus - 0.5
    log_decay = -jnp.exp(w_log)
    a = jax.nn.sigmoid(a0_ref[...] + jnp.dot(tail.astype(BF16), a2_ref[...], preferred_element_type=F32))
    g = jnp.dot(jax.nn.sigmoid(tail).astype(BF16), g2_ref[...], preferred_element_type=F32)
    kk = k * kk_ref[...]
    n2 = _head_sum(kk * kk, bo_ref[...])
    kk = kk / jnp.maximum(jnp.sqrt(n2), 1e-12)
    kx = k * (1.0 + (a - 1.0) * ka_ref[...])
    r_out[...] = r
    lw_out[...] = log_decay
    kx_out[...] = kx
    v_out[...] = v
    kkn_out[...] = kk
    b_out[...] = kk * a
    g_out[...] = g


def rwkv_prep(u, first, consts, *, m0, m, tm, period):
    mu, w0, a0, k_k, k_a, w2p, a2p, g2p, bo = consts
    W = RWKV_WIDTH
    nt = m // tm
    b0 = m0 // tm
    assert m % tm == 0 and m0 % tm == 0
    first_rows = first.shape[0] // nt
    row_spec = lambda width: pl.BlockSpec((1, width), lambda i: (0, 0))
    full = lambda a: pl.BlockSpec(a.shape, lambda i: (0, 0))
    out_tile = pl.BlockSpec((tm, W), lambda i: (i, 0))
    return pl.pallas_call(
        functools.partial(_rwkv_prep_kernel, period=period),
        grid=(nt,),
        in_specs=[
            pl.BlockSpec((tm, UA_PAD), lambda i: (i + b0, OFF_UA // UA_PAD)),
            pl.BlockSpec((first_rows, UA_PAD), lambda i: (i, 0)),
            row_spec(UA_PAD), row_spec(W), row_spec(W), row_spec(W), row_spec(W),
            full(w2p), full(a2p), full(g2p), full(bo),
        ],
        out_specs=[out_tile] * 7,
        out_shape=[jax.ShapeDtypeStruct((m, W), F32)] * 7,
        compiler_params=_cparams("parallel"),
        name="rwkv_prep",
    )(u, first, mu, w0, a0, k_k, k_a, w2p, a2p, g2p, bo)


RWKV_CHUNK = 64
INV_BASE = 16
TN_DIMS = (((0,), (0,)), ((), ()))


def _mm3(a, b, dims=None):
    ah, al = _split2(a)
    bh, bl = _split2(b)
    if dims is None:
        dot = lambda x, y: jnp.dot(x, y, preferred_element_type=F32)
    else:
        dot = lambda x, y: lax.dot_general(x, y, dims, preferred_element_type=F32)
    return dot(ah, bh) + dot(ah, bl) + dot(al, bh)


def _mm_exact_lhs(a_bf, b):
    hi, mid, lo = _split3(b)
    out = jnp.dot(a_bf, hi, preferred_element_type=F32)
    out += jnp.dot(a_bf, mid, preferred_element_type=F32)
    out += jnp.dot(a_bf, lo, preferred_element_type=F32)
    return out


def _rwkv_chunk_kernel(r_ref, lw_ref, kx_ref, kk_ref, b_ref, v_ref, s0_ref, y_ref, sout_ref, s_scr, *, n_tb):
    tb = pl.program_id(1)
    C = RWKV_CHUNK
    R2 = 2 * C

    @pl.when(tb == 0)
    def _():
        s_scr[...] = s0_ref[0]

    ri = lax.broadcasted_iota(jnp.int32, (R2, R2), 0)
    ci = lax.broadcasted_iota(jnp.int32, (R2, R2), 1)
    same_head = (ri // C) == (ci // C)
    lower_strict = same_head & ((ri % C) > (ci % C))
    lower_incl = same_head & ((ri % C) >= (ci % C))
    eye = ri == ci
    same_base = (ri // INV_BASE) == (ci // INV_BASE)
    tri = jnp.where((lax.broadcasted_iota(jnp.int32, (C, C), 0) >= lax.broadcasted_iota(jnp.int32, (C, C), 1)),
                    1.0, 0.0).astype(BF16)
    lane = lax.broadcasted_iota(jnp.int32, (C, LANES), 1)
    left = lane < HALF
    lane_s = lax.broadcasted_iota(jnp.int32, (HALF, LANES), 1)
    left_s = lane_s < HALF

    def stack(x):
        return jnp.concatenate([jnp.where(left, x, 0.0), jnp.where(left, 0.0, x)], axis=0)

    def tile2(x):
        return jnp.concatenate([x, x], axis=0)

    slabs = range(RWKV_HEADS // 2)
    cols = [slice(p * LANES, (p + 1) * LANES) for p in slabs]
    each = lambda f, *xs: [f(*a) for a in zip(*xs)]
    r = [r_ref[0, :, cs] for cs in cols]
    lw = [lw_ref[0, :, cs] for cs in cols]
    kx = [kx_ref[0, :, cs] for cs in cols]
    kk = [kk_ref[0, :, cs] for cs in cols]
    b = [b_ref[0, :, cs] for cs in cols]
    v = [v_ref[0, :, cs] for cs in cols]
    cl = each(lambda x: _mm_exact_lhs(tri, x), lw)
    cl_end = each(lambda x: x[C - 1:C], cl)
    g_inv = each(lambda x: jnp.exp(-x), cl)
    g_end = each(lambda e, x: jnp.exp(e - x), cl_end, cl)
    kt = each(lambda k_, c_, l_: k_ * jnp.exp(c_ - l_), kk, cl, lw)
    rt = each(lambda r_, c_: r_ * jnp.exp(c_), r, cl)
    bh = each(jnp.multiply, b, g_inv)
    kh = each(jnp.multiply, kx, g_inv)
    bbar = each(jnp.multiply, b, g_end)
    kbar = each(jnp.multiply, kx, g_end)
    gram = each(lambda kt_, rt_, bh_, kh_: _mm3(jnp.concatenate([kt_, rt_], axis=0),
                                                jnp.concatenate([stack(bh_), stack(kh_)], axis=0), NT_DIMS),
                kt, rt, bh, kh)
    n_mat = each(lambda g_: jnp.where(lower_strict, tile2(g_[0:C, 0:R2]), 0.0), gram)
    ak = each(lambda g_: jnp.where(lower_strict, tile2(g_[0:C, R2:2 * R2]), 0.0), gram)
    bb = each(lambda g_: jnp.where(lower_incl, tile2(g_[C:R2, 0:R2]), 0.0), gram)
    bk = each(lambda g_: jnp.where(lower_incl, tile2(g_[C:R2, R2:2 * R2]), 0.0), gram)
    d1 = each(lambda n_: jnp.where(same_base, n_, 0.0), n_mat)
    d2 = each(_mm3, d1, d1)
    d4 = each(_mm3, d2, d2)
    d8 = each(_mm3, d4, d4)
    t_inv = each(lambda d_: jnp.where(eye, 1.0, 0.0) - d_, d1)
    for dk in (d2, d4, d8):
        t_inv = each(lambda t_, d_: t_ + _mm3(t_, d_), t_inv, dk)
    blk = INV_BASE
    while blk < C:
        off_mask = ((ri // (2 * blk)) == (ci // (2 * blk))) & ((ri // blk) != (ci // blk))
        tmp = each(lambda t_, n_: _mm3(t_, jnp.where(off_mask, n_, 0.0)), t_inv, n_mat)
        t_inv = each(lambda t_, m_: t_ - _mm3(m_, t_), t_inv, tmp)
        blk *= 2
    v_st = each(stack, v)
    kt_st = each(stack, kt)
    w_st = each(_mm3, ak, v_st)
    pq = each(lambda t_, k_, w_: _mm3(t_, jnp.concatenate([k_, w_], axis=1)), t_inv, kt_st, w_st)
    z = each(lambda bb_, bk_, pq_, v_: _mm3(
        jnp.concatenate([-bb_, bk_], axis=1),
        jnp.concatenate([pq_, jnp.concatenate([jnp.zeros_like(v_), v_], axis=1)], axis=0)), bb, bk, pq, v_st)
    ry_st = each(lambda rt_, z_: stack(rt_) + z_[:, 0:LANES], rt, z)
    bbar_st = each(stack, bbar)
    g_mat = each(lambda e_, pq_, bs_: jnp.where(eye, jnp.broadcast_to(jnp.exp(e_), (R2, LANES)), 0.0)
                 - _mm3(pq_[:, 0:LANES].T, bs_), cl_end, pq, bbar_st)
    h_bd = each(lambda v_, pq_, kb_, bs_: _mm3(jnp.concatenate([v_, pq_[:, LANES:2 * LANES]], axis=0).T,
                                               jnp.concatenate([stack(kb_), -bs_], axis=0)), v_st, pq, kbar, bbar_st)
    for p in slabs:
        s_old = s_scr[p]
        s_dup = jnp.concatenate([jnp.where(left_s, s_old, 0.0), jnp.where(left_s, 0.0, s_old)], axis=0)
        y_st = _mm3(ry_st[p], s_dup, NT_DIMS) + z[p][:, LANES:2 * LANES]
        y_ref[0, :, cols[p]] = y_st[0:C] + y_st[C:R2]
        s_scr[p] = _mm3(s_old, g_mat[p]) + (h_bd[p][0:HALF] + h_bd[p][HALF:LANES])

    @pl.when(tb == n_tb - 1)
    def _():
        sout_ref[0] = s_scr[...]


def rwkv_chunk_scan(r, lw, kx, kk, b, v, s0_slabs):
    bsz, t, W = r.shape
    n_tb = t // RWKV_CHUNK
    assert t % RWKV_CHUNK == 0
    seq = pl.BlockSpec((1, RWKV_CHUNK, W), lambda bi, ti: (bi, ti, 0))
    st = pl.BlockSpec((1, RWKV_HEADS // 2, 64, LANES), lambda bi, ti: (bi, 0, 0, 0))
    return pl.pallas_call(
        functools.partial(_rwkv_chunk_kernel, n_tb=n_tb),
        grid=(bsz, n_tb),
        in_specs=[seq] * 6 + [st],
        out_specs=[seq, st],
        out_shape=[jax.ShapeDtypeStruct((bsz, t, W), F32),
                   jax.ShapeDtypeStruct((bsz, RWKV_HEADS // 2, 64, LANES), F32)],
        scratch_shapes=[pltpu.VMEM((RWKV_HEADS // 2, 64, LANES), F32)],
        compiler_params=_cparams("parallel", "arbitrary"),
        name="rwkv_chunk_scan",
    )(r, lw, kx, kk, b, v, s0_slabs)


def _rwkv_post_kernel(y_ref, r_ref, kx_ref, v_ref, g_ref, rk_ref, lnw_ref, lnb_ref, bo_ref, o_ref):
    bo = bo_ref[...]
    y = y_ref[...]
    inv_n = 1.0 / RWKV_HEAD_SIZE
    ym = _head_sum(y, bo) * inv_n
    d = y - ym
    yv = _head_sum(d * d, bo) * inv_n
    yn = d * lax.rsqrt(yv + RWKV_GN_EPS) * lnw_ref[...] + lnb_ref[...]
    bonus = _head_sum(r_ref[...] * kx_ref[...] * rk_ref[...], bo) * v_ref[...]
    o_ref[...] = ((yn + bonus) * g_ref[...]).astype(BF16)


def rwkv_post(y, r, kx, v, g, r_k, ln_w, ln_b, bo, *, tm):
    m, W = y.shape
    tile = pl.BlockSpec((tm, W), lambda i: (i, 0))
    row = pl.BlockSpec((1, W), lambda i: (0, 0))
    return pl.pallas_call(
        _rwkv_post_kernel,
        grid=(m // tm,),
        in_specs=[tile] * 5 + [row] * 3 + [pl.BlockSpec(bo.shape, lambda i: (0, 0))],
        out_specs=tile,
        out_shape=jax.ShapeDtypeStruct((m, W), BF16),
        compiler_params=_cparams("parallel"),
        name="rwkv_post",
    )(y, r, kx, v, g, r_k, ln_w, ln_b, bo)


def rwkv_consts(mu, w0, w2, a0, a2, g2, k_k, k_a):
    W = RWKV_WIDTH
    tail = UA_PAD - 3 * W
    mu_p = jnp.zeros((1, UA_PAD), F32).at[0, :RWKV_COLS].set(mu.astype(F32))
    o_a = DECAY_LORA
    o_g = DECAY_LORA + AAA_LORA
    w2p = jnp.zeros((tail, W), BF16).at[0:o_a].set(w2.astype(BF16))
    a2p = jnp.zeros((tail, W), BF16).at[o_a:o_g].set(a2.astype(BF16))
    g2p = jnp.zeros((tail, W), BF16).at[o_g:o_g + GATE_LORA].set(g2.astype(BF16))
    half = jnp.arange(LANES) // RWKV_HEAD_SIZE
    bo = (half[:, None] == half[None, :]).astype(BF16)
    row = lambda p: p.astype(F32).reshape(1, W)
    return (mu_p, row(w0), row(a0), row(k_k), row(k_a), w2p, a2p, g2p, bo)


def _state_to_slabs(s):
    bsz = s.shape[0]
    hp = RWKV_HEADS // 2
    return s.reshape(bsz, hp, 2, 64, 64).transpose(0, 1, 3, 2, 4).reshape(bsz, hp, 64, LANES)


def _slabs_to_state(s):
    bsz = s.shape[0]
    hp = RWKV_HEADS // 2
    return s.reshape(bsz, hp, 64, 2, 64).transpose(0, 1, 3, 2, 4).reshape(bsz, RWKV_HEADS, 64, 64)


PREP_TM = 256


def rwkv_time_mix(u, m0, m, bsz, t, shift_state, s0, consts, r_k, ln_w, ln_b, *, prompt):
    W = RWKV_WIDTH
    bo = consts[-1]
    if prompt:
        tm = min(PREP_TM, m)
        nt = m // tm
        prev_rows = u[m0 + tm - 1:m0 + m - 1:tm, OFF_UA:OFF_UA + UA_PAD]
        rows = jnp.concatenate([jnp.zeros((1, UA_PAD), F32), prev_rows], axis=0)
        first = jnp.zeros((nt, SUBLANES, UA_PAD), F32).at[:, 0].set(rows).reshape(nt * SUBLANES, UA_PAD)
        period = tm
        s0_slabs = jnp.zeros((bsz, RWKV_HEADS // 2, 64, LANES), F32)
    else:
        tm = m
        sp = jnp.zeros((bsz, UA_PAD), F32).at[:, :RWKV_COLS].set(shift_state.astype(F32))
        first = jnp.repeat(sp, t, axis=0)
        period = t
        s0_slabs = _state_to_slabs(s0.astype(F32))
    r, lw, kx, v, kk, b, g = rwkv_prep(u, first, consts, m0=m0, m=m, tm=tm, period=period)
    tpad = -(-t // RWKV_CHUNK) * RWKV_CHUNK
    seq = lambda a: jnp.pad(a.reshape(bsz, t, W), ((0, 0), (0, tpad - t), (0, 0)))
    y3, s_fin = rwkv_chunk_scan(seq(r), seq(lw), seq(kx), seq(kk), seq(b), seq(v), s0_slabs)
    y = y3[:, :t].reshape(m, W)
    za = rwkv_post(y, r, kx, v, g, r_k.astype(F32).reshape(1, W), ln_w.astype(F32).reshape(1, W),
                   ln_b.astype(F32).reshape(1, W), bo, tm=tm)
    return za, _slabs_to_state(s_fin)


KV_ROW = 2 * NSA_KV_HEADS * HEAD_DIM
CHUNKS_PER_PAGE = PAGE_SIZE // CMP_STRIDE
PROJ_W = CMP_RANK * KV_ROW


PAGES_PER_STEP = 16


def _page_specs(block, n_pages):
    def spec(i):
        return pl.BlockSpec(block, lambda b, p, pt: (pt[b * n_pages + p * PAGES_PER_STEP + i],) + (0,) * (len(block) - 1))
    return [spec(i) for i in range(PAGES_PER_STEP)]


PAGE_BLOCK = (1, PAGE_SIZE, KV_ROW)


def _cmp_proj_kernel(pt_ref, *refs):
    del pt_ref
    x_refs = refs[:PAGES_PER_STEP]
    w_ref, f_ref, s_ref = refs[PAGES_PER_STEP:]
    half = KV_ROW // 2
    n_out = CMP_RANK * CHUNKS_PER_PAGE
    for i, x_ref in enumerate(x_refs):
        rows = slice(i * CHUNKS_PER_PAGE, (i + 1) * CHUNKS_PER_PAGE)
        for s in range(2):
            out = jnp.dot(w_ref[s], x_ref[0, :, s * half:(s + 1) * half].astype(BF16), preferred_element_type=F32)
            for r in range(CMP_RANK):
                cols = slice(r * KV_ROW + s * half, r * KV_ROW + (s + 1) * half)
                f_ref[0, rows, cols] = out[r * CHUNKS_PER_PAGE:(r + 1) * CHUNKS_PER_PAGE]
                s_ref[0, rows, cols] = out[n_out + r * CHUNKS_PER_PAGE:n_out + (r + 1) * CHUNKS_PER_PAGE]


def cmp_project(pages, page_table_flat, bsz, n_pages, w_proj):
    assert n_pages % PAGES_PER_STEP == 0
    n_ch = n_pages * CHUNKS_PER_PAGE
    step_ch = PAGES_PER_STEP * CHUNKS_PER_PAGE
    out = jax.ShapeDtypeStruct((bsz, n_ch, PROJ_W), F32)
    o_spec = pl.BlockSpec((1, step_ch, PROJ_W), lambda b, p, pt: (b, p, 0))
    w_spec = pl.BlockSpec(w_proj.shape, lambda b, p, pt: (0, 0, 0))
    return pl.pallas_call(
        _cmp_proj_kernel,
        grid_spec=pltpu.PrefetchScalarGridSpec(
            num_scalar_prefetch=1,
            grid=(bsz, n_pages // PAGES_PER_STEP),
            in_specs=_page_specs(PAGE_BLOCK, n_pages) + [w_spec],
            out_specs=[o_spec, o_spec],
        ),
        out_shape=[out, out],
        compiler_params=_cparams("parallel", "arbitrary"),
        name="cmp_project",
    )(page_table_flat, *([pages] * PAGES_PER_STEP), w_proj)


def _cmp_mix_kernel(f_ref, s_ref, waf_ref, was_ref, pex_ref, wb_ref, o_ref):
    n_ch = f_ref.shape[1]
    acc = jnp.zeros((n_ch, KV_ROW), F32)
    for r in range(CMP_RANK):
        rs = slice(r * CMP_STRIDE, (r + 1) * CMP_STRIDE)
        pe = jnp.sum(waf_ref[rs, :] * pex_ref[0:CMP_STRIDE, :], axis=0, keepdims=True)
        pe += jnp.sum(was_ref[rs, :] * pex_ref[CMP_STRIDE:CMP_BLOCK, :], axis=0, keepdims=True)
        cs = slice(r * KV_ROW, (r + 1) * KV_ROW)
        nxt = pltpu.roll(s_ref[0, :, cs], n_ch - 1, axis=0)
        hid = jax.nn.gelu(f_ref[0, :, cs] + nxt + pe)
        acc += jnp.dot(hid.astype(BF16), wb_ref[r], preferred_element_type=F32)
    o_ref[0] = acc.astype(BF16)


def cmp_mix(f, s, waf, was, pex, wb_bd):
    bsz, n_ch, _ = f.shape
    io = pl.BlockSpec((1, n_ch, PROJ_W), lambda b: (b, 0, 0))
    full2 = lambda a: pl.BlockSpec(a.shape, lambda b: (0, 0))
    return pl.pallas_call(
        _cmp_mix_kernel,
        grid=(bsz,),
        in_specs=[io, io, full2(waf), full2(was), full2(pex), pl.BlockSpec(wb_bd.shape, lambda b: (0, 0, 0))],
        out_specs=pl.BlockSpec((1, n_ch, KV_ROW), lambda b: (b, 0, 0)),
        out_shape=jax.ShapeDtypeStruct((bsz, n_ch, KV_ROW), BF16),
        compiler_params=_cparams("parallel"),
        name="cmp_mix",
    )(f, s, waf, was, pex, wb_bd)


def cmp_consts(cmp_pe, cmp_wa, cmp_wb):
    hd = NSA_KV_HEADS * HEAD_DIM
    wa_cols = jnp.repeat(cmp_wa.astype(F32).transpose(2, 1, 0), hd, axis=2)
    waf = wa_cols[:, :CMP_STRIDE].reshape(CMP_RANK * CMP_STRIDE, KV_ROW)
    was = wa_cols[:, CMP_STRIDE:].reshape(CMP_RANK * CMP_STRIDE, KV_ROW)
    pex = jnp.broadcast_to(cmp_pe.astype(F32).transpose(1, 0, 2)[:, :, None, :],
                           (CMP_BLOCK, 2, NSA_KV_HEADS, HEAD_DIM)).reshape(CMP_BLOCK, KV_ROW)
    eye = jnp.eye(NSA_KV_HEADS, dtype=F32)
    wb = cmp_wb.astype(F32)
    bd = jnp.einsum('srde,hg,st->rshdtge', wb, eye, jnp.eye(2, dtype=F32)).reshape(CMP_RANK, KV_ROW, KV_ROW)
    wa4 = cmp_wa.astype(F32).reshape(2, 2, CMP_STRIDE, CMP_RANK)
    w_proj = jnp.einsum('sflr,nm->sfrnml', wa4, jnp.eye(CHUNKS_PER_PAGE, dtype=F32)).reshape(
        2, 2 * CMP_RANK * CHUNKS_PER_PAGE, PAGE_SIZE)
    return waf, was, pex, bd.astype(BF16), w_proj.astype(BF16)


def compress(pages, page_table_flat, bsz, n_pages, consts):
    waf, was, pex, wb_bd, w_proj = consts
    f, s = cmp_project(pages, page_table_flat, bsz, n_pages, w_proj)
    kvc = cmp_mix(f, s, waf, was, pex, wb_bd)
    ratio = SEL_BLOCK // CMP_STRIDE
    n_ch = LANES * ratio
    kvc = jnp.pad(kvc, ((0, 0), (0, n_ch - kvc.shape[1]), (0, 0)))
    return kvc.reshape(bsz, LANES, ratio, KV_ROW).transpose(0, 2, 1, 3).reshape(bsz, n_ch, KV_ROW)


NT_DIMS = (((1,), (1,)), ((), ()))
HALF = LANES // 2
KEY_TILE = 512
WIN_SPAN = WINDOW + Q_BLOCK
TOPK_UNROLL = 8


def _masked_softmax_rows(s, ok):
    x = jnp.where(ok, s, NEG_INF)
    m = jnp.max(x, axis=-1, keepdims=True)
    e = jnp.exp(x - m)
    l = jnp.sum(e, axis=-1, keepdims=True)
    return jnp.where(ok, e / l, 0.0)


def _top_blocks(score, n_keep):
    lane = lax.broadcasted_iota(jnp.int32, score.shape, 1)

    def body(d, cnt):
        rolled = pltpu.roll(score, d, axis=1)
        beats = (rolled > score) | ((rolled == score) & (lane >= d))
        return cnt + jnp.where(beats, 1.0, 0.0)

    cnt = lax.fori_loop(1, LANES, body, jnp.zeros(score.shape, F32), unroll=TOPK_UNROLL)
    return cnt < n_keep


def _rank_unselected(st_ref, cnt_ref, n_keep, n_live):
    n = st_ref.shape[1]
    sub = lax.broadcasted_iota(jnp.int32, (SUBLANES, LANES), 0)
    n_tiles = LANES // SUBLANES
    cnt_ref[...] = jnp.zeros(cnt_ref.shape, F32)
    for jg in range(n_tiles):

        @pl.when(jg * SUBLANES < n_live)
        def _():
            for c in range(n // LANES):
                cs = slice(c * LANES, (c + 1) * LANES)
                rows = [st_ref[j:j + 1, cs] for j in range(jg * SUBLANES, (jg + 1) * SUBLANES)]
                for t in range(n_tiles):
                    ts_ = slice(t * SUBLANES, (t + 1) * SUBLANES)
                    si = st_ref[ts_, cs]
                    cnt = cnt_ref[ts_, cs]
                    for jj, sj in enumerate(rows):
                        if t > jg:
                            beats = sj >= si
                        elif t < jg:
                            beats = sj > si
                        else:
                            beats = (sj > si) | ((sj == si) & (sub > jj))
                        cnt = cnt + jnp.where(beats, 1.0, 0.0)
                    cnt_ref[ts_, cs] = cnt
    return jnp.where(cnt_ref[...] < n_keep, 0.0, 1.0).astype(BF16)


def _col_softmax(x):
    m = jnp.max(x, axis=0, keepdims=True)
    e = jnp.exp(x - m)
    l = jnp.sum(e, axis=0, keepdims=True)
    return e * (1.0 / l), m, l


def _nsa_prompt_kernel(q_ref, gn_ref, kc_ref, vct_ref, ks_ref, vst_ref, kw_ref, vwt_ref, o_ref, st_scr, cnt_scr):
    i = pl.program_id(0)
    start = i * Q_BLOCK
    assert math.log2(ATTN_SCALE).is_integer()
    q = q_ref[...] * ATTN_SCALE
    gates_t = jax.nn.sigmoid(gn_ref[...]).T
    pos = start + lax.broadcasted_iota(jnp.int32, (1, Q_BLOCK), 1)
    n_cmp = kc_ref.shape[0]
    ratio = SEL_BLOCK // CMP_STRIDE
    crow = lax.broadcasted_iota(jnp.int32, (n_cmp, 1), 0)
    cmp_idx = (crow % LANES) * ratio + crow // LANES
    cmp_ok = (cmp_idx * CMP_STRIDE + (CMP_BLOCK - 1)) <= pos
    cmp_bias = jnp.where(cmp_ok, 0.0, NEG_INF)
    blk = lax.broadcasted_iota(jnp.int32, (LANES, 1), 0)
    cur = pos // SEL_BLOCK
    forced = (blk == 0) | (blk == cur) | (blk == cur - 1)
    future = blk * SEL_BLOCK > pos
    key_sub = lax.broadcasted_iota(jnp.int32, (KEY_TILE, 1), 0)
    blk_lane = lax.broadcasted_iota(jnp.int32, (1, LANES), 1)
    lane_q = lax.broadcasted_iota(jnp.int32, (Q_BLOCK, LANES), 1)
    n_tiles = start // KEY_TILE + 1
    wb0 = jnp.maximum(start - WINDOW, 0) // Q_BLOCK
    n_wblk = WIN_SPAN // Q_BLOCK

    heads = range(NSA_KV_HEADS)
    groups = range(NSA_GROUP)
    ksls = [slice((h // 2) * LANES, (h // 2 + 1) * LANES) for h in heads]
    keeps = [(lane_q // HALF) == (h % 2) for h in heads]
    qs = []
    for h in heads:
        row = []
        for g in groups:
            head = h * NSA_GROUP + g
            slab = q[:, (head // 2) * LANES:(head // 2 + 1) * LANES]
            if head % 2 != h % 2:
                slab = pltpu.roll(slab, HALF, axis=1)
            row.append(jnp.where(keeps[h], slab, 0.0).astype(BF16))
        qs.append(row)

    pairs = [(h, g) for h in heads for g in groups]
    cs_ = [lax.dot_general(kc_ref[:, ksls[h]], qs[h][g], NT_DIMS, preferred_element_type=F32) + cmp_bias
           for h, g in pairs]
    pcs = [jnp.where(cmp_ok, _col_softmax(x)[0], 0.0) for x in cs_]
    o_c = [jnp.dot(vct_ref[ksls[h], :], pcs[i].astype(BF16), preferred_element_type=F32)
           for i, (h, g) in enumerate(pairs)]
    for h in heads:
        imp = pcs[h * NSA_GROUP]
        for g in range(1, NSA_GROUP):
            imp = imp + pcs[h * NSA_GROUP + g]
        imp_blk = imp[0:LANES]
        for j in range(1, ratio):
            imp_blk = imp_blk + imp[j * LANES:(j + 1) * LANES]
        st_scr[:, h * Q_BLOCK:(h + 1) * Q_BLOCK] = jnp.where(future, -FORCE, jnp.where(forced, FORCE, imp_blk))
    n_live = (start + Q_BLOCK - 1) // SEL_BLOCK + 1
    unsel = _rank_unselected(st_scr, cnt_scr, N_SEL, n_live)

    def sel_tile(kt, carry, causal):
        k0 = kt * KEY_TILE
        expand = jnp.where((k0 + key_sub) // SEL_BLOCK == blk_lane, NEG_INF, 0.0).astype(BF16)
        biases = [jnp.dot(expand, unsel[:, h * Q_BLOCK:(h + 1) * Q_BLOCK], preferred_element_type=F32) for h in heads]
        if causal:
            visible = (k0 + key_sub) <= pos
            biases = [jnp.where(visible, b_, NEG_INF) for b_ in biases]
        biases2 = [jnp.concatenate([b_, b_], axis=1) for b_ in biases]
        xs = [lax.dot_general(ks_ref[kt, :, ksls[h]], q2[h][gp], NT_DIMS, preferred_element_type=F32) + biases2[h]
              for h, gp in duos]
        m_new = [jnp.maximum(carry[i][0], jnp.max(x, axis=0, keepdims=True)) for i, x in enumerate(xs)]
        ps = [jnp.exp(x - m) for x, m in zip(xs, m_new)]
        scs = [jnp.exp(carry[i][0] - m) for i, m in enumerate(m_new)]
        l_new = [sc * carry[i][1] + jnp.sum(p, axis=0, keepdims=True) for i, (sc, p) in enumerate(zip(scs, ps))]
        pvs = [jnp.dot(vst_ref[kt, ksls[h], :], ps[i].astype(BF16), preferred_element_type=F32)
               for i, (h, gp) in enumerate(duos)]
        return tuple((m_new[i], l_new[i], scs[i] * carry[i][2] + pvs[i]) for i in range(len(duos)))

    duos = [(h, gp) for h in heads for gp in range(NSA_GROUP // 2)]
    q2 = [[jnp.concatenate([qs[h][2 * gp], qs[h][2 * gp + 1]], axis=0) for gp in range(NSA_GROUP // 2)] for h in heads]
    init = tuple((jnp.full((1, 2 * Q_BLOCK), NEG_INF, F32), jnp.zeros((1, 2 * Q_BLOCK), F32),
                  jnp.zeros((LANES, 2 * Q_BLOCK), F32)) for _ in duos)
    carry2 = lax.fori_loop(0, n_tiles - 1, functools.partial(sel_tile, causal=False), init)
    carry2 = sel_tile(n_tiles - 1, carry2, True)
    carry = []
    for _, l2, acc2 in carry2:
        for half in range(2):
            hs_ = slice(half * Q_BLOCK, (half + 1) * Q_BLOCK)
            carry.append((None, l2[:, hs_], acc2[:, hs_]))

    kpos = wb0 * Q_BLOCK + lax.broadcasted_iota(jnp.int32, (WIN_SPAN, 1), 0)
    delta = pos - kpos
    win_bias = jnp.where((delta >= 0) & (delta < WINDOW), 0.0, NEG_INF)
    k_ws = [jnp.concatenate([kw_ref[wb0 + j, :, ksls[h]] for j in range(n_wblk)], axis=0) for h in heads]
    ws_ = [lax.dot_general(k_ws[h], qs[h][g], NT_DIMS, preferred_element_type=F32) + win_bias for h, g in pairs]
    pws = [_col_softmax(x)[0].astype(BF16) for x in ws_]
    o_w = []
    for i, (h, g) in enumerate(pairs):
        acc = jnp.dot(vwt_ref[wb0, ksls[h], :], pws[i][0:Q_BLOCK], preferred_element_type=F32)
        for j in range(1, n_wblk):
            acc += jnp.dot(vwt_ref[wb0 + j, ksls[h], :], pws[i][j * Q_BLOCK:(j + 1) * Q_BLOCK],
                           preferred_element_type=F32)
        o_w.append(acc)
    pieces = []
    for head, (h, g) in enumerate(pairs):
        _, l_s, acc_s = carry[head]
        o_s = acc_s * (1.0 / l_s)
        mix = (gates_t[head:head + 1] * o_c[head] + gates_t[NSA_HEADS + head:NSA_HEADS + head + 1] * o_s
               + gates_t[2 * NSA_HEADS + head:2 * NSA_HEADS + head + 1] * o_w[head])
        pieces.append(mix[(h % 2) * HALF:(h % 2 + 1) * HALF])
    o_t = jnp.concatenate(pieces, axis=0)
    o_ref[...] = o_t.T.astype(BF16)


def nsa_prompt(u, t, kvc, sel_kv_bf, win_kv_bf):
    n_qb = t // Q_BLOCK
    hw = NSA_KV_HEADS * HEAD_DIM
    tiles = lambda a, rows: a.reshape(t // rows, rows, hw)
    tiles_t = lambda a, rows: a.reshape(t // rows, rows, hw).transpose(0, 2, 1)
    operands = [
        kvc[:, :hw], kvc[:, hw:].T,
        tiles(sel_kv_bf[:, :hw], KEY_TILE), tiles_t(sel_kv_bf[:, hw:], KEY_TILE),
        tiles(win_kv_bf[:, :hw], Q_BLOCK), tiles_t(win_kv_bf[:, hw:], Q_BLOCK),
    ]
    whole = lambda a: pl.BlockSpec(a.shape, lambda i: (0,) * a.ndim)
    return pl.pallas_call(
        _nsa_prompt_kernel,
        grid=(n_qb,),
        in_specs=[
            pl.BlockSpec((Q_BLOCK, Q_COLS), lambda i: (i, OFF_Q // Q_COLS)),
            pl.BlockSpec((Q_BLOCK, GN_PAD), lambda i: (i, OFF_GN // GN_PAD)),
        ] + [whole(a) for a in operands],
        out_specs=pl.BlockSpec((Q_BLOCK, NSA_WIDTH), lambda i: (i, 0)),
        out_shape=jax.ShapeDtypeStruct((t, NSA_WIDTH), BF16),
        scratch_shapes=[pltpu.VMEM((LANES, NSA_KV_HEADS * Q_BLOCK), F32)] * 2,
        compiler_params=_cparams("arbitrary"),
        name="nsa_prompt",
    )(u, u, *operands)


def _tail_keys(qf, k_new, v_new, tok, m, l, acc):
    nt = k_new.shape[0]
    kn = k_new.astype(BF16).astype(F32)
    vn = v_new.astype(BF16).astype(F32)
    s = [jnp.sum(qf * kn[j:j + 1], axis=1, keepdims=True) * ATTN_SCALE for j in range(nt)]
    ok = [tok >= j for j in range(nt)]
    m_new = m
    for j in range(nt):
        m_new = jnp.maximum(m_new, jnp.where(ok[j], s[j], NEG_INF))
    scale = jnp.exp(m - m_new)
    l = scale * l
    acc = scale * acc
    for j in range(nt):
        p = jnp.where(ok[j], jnp.exp(s[j] - m_new), 0.0)
        l = l + p
        acc = acc + p * vn[j:j + 1]
    return m_new, l, acc


def _nsa_sample_kernel(pt_ref, q_ref, gn_ref, kc_ref, vc_ref, *refs, nt, past, n_steps):
    del pt_ref
    page_refs = refs[:PAGES_PER_STEP]
    (snew_ref, wnew_ref, win_ref, o_ref,
     qh_scr, sel_scr, oc_scr, m_scr, l_scr, acc_scr, kb_scr, vb_scr) = refs[PAGES_PER_STEP:]
    p = pl.program_id(1)
    rows = NSA_GROUP * nt
    hw = NSA_KV_HEADS * HEAD_DIM
    row1 = lax.broadcasted_iota(jnp.int32, (rows, 1), 0)
    tok = row1 % nt
    grp = row1 // nt
    pos = past + tok
    lane = lax.broadcasted_iota(jnp.int32, (rows, LANES), 1)
    blk = lax.broadcasted_iota(jnp.int32, (1, LANES), 1)
    ratio = SEL_BLOCK // CMP_STRIDE

    @pl.when(p == 0)
    def _():
        q16 = q_ref[0]
        n_cmp = kc_ref.shape[1]
        col = lax.broadcasted_iota(jnp.int32, (1, n_cmp), 1)
        cmp_idx = (col % LANES) * ratio + col // LANES
        cmp_ok = (cmp_idx * CMP_STRIDE + (CMP_BLOCK - 1)) <= pos
        cur = pos // SEL_BLOCK
        forced = (blk == 0) | (blk == cur) | (blk == cur - 1)
        future = blk * SEL_BLOCK > pos
        for h in range(NSA_KV_HEADS):
            khalf = h % 2
            ksl = slice((h // 2) * LANES, (h // 2 + 1) * LANES)
            keep = (lane // HALF) == khalf
            qh = jnp.zeros((rows, LANES), F32)
            for g in range(NSA_GROUP):
                head = h * NSA_GROUP + g
                slab = q16[:, (head // 2) * LANES:(head // 2 + 1) * LANES]
                if head % 2 != khalf:
                    slab = pltpu.roll(slab, HALF, axis=1)
                qh = jnp.where((grp == g) & keep, slab, qh)
            qh_scr[h] = qh
            sc = lax.dot_general(qh.astype(BF16), kc_ref[0, :, ksl], NT_DIMS, preferred_element_type=F32) * ATTN_SCALE
            pc = _masked_softmax_rows(sc, cmp_ok)
            oc_scr[h] = jnp.dot(pc.astype(BF16), vc_ref[0, :, ksl], preferred_element_type=F32)
            imp = pc
            for g in range(1, NSA_GROUP):
                imp = imp + pltpu.roll(pc, g * nt, axis=0)
            imp_blk = imp[:, 0:LANES]
            for j in range(1, ratio):
                imp_blk = imp_blk + imp[:, j * LANES:(j + 1) * LANES]
            score = jnp.where(future, -FORCE, jnp.where(forced, FORCE, imp_blk))
            sel_scr[h] = jnp.where(_top_blocks(score, N_SEL - 1), 1.0, 0.0)
            m_scr[h] = jnp.full((rows, 1), NEG_INF, F32)
            l_scr[h] = jnp.zeros((rows, 1), F32)
            acc_scr[h] = jnp.zeros((rows, LANES), F32)

    step_keys = PAGES_PER_STEP * PAGE_SIZE
    for i, page_ref in enumerate(page_refs):
        ks = slice(i * PAGE_SIZE, (i + 1) * PAGE_SIZE)
        kb_scr[ks, :] = page_ref[0, :, 0:hw].astype(BF16)
        vb_scr[ks, :] = page_ref[0, :, hw:2 * hw].astype(BF16)
    blk_row = lax.broadcasted_iota(jnp.int32, (LANES, 1), 0)
    key_lane = lax.broadcasted_iota(jnp.int32, (1, step_keys), 1)
    expand = jnp.where(blk_row == (p * step_keys + key_lane) // SEL_BLOCK, 1.0, 0.0).astype(BF16)
    for slab in range(NSA_KV_HEADS // 2):
        h0, h1 = 2 * slab, 2 * slab + 1
        ksl = slice(slab * LANES, (slab + 1) * LANES)
        pair = lambda ref: jnp.concatenate([ref[h0], ref[h1]], axis=0)
        s = lax.dot_general(pair(qh_scr).astype(BF16), kb_scr[:, ksl], NT_DIMS,
                            preferred_element_type=F32) * ATTN_SCALE
        ok = jnp.dot(pair(sel_scr).astype(BF16), expand, preferred_element_type=F32) > 0.5
        x = jnp.where(ok, s, NEG_INF)
        m_i = pair(m_scr)
        m_new = jnp.maximum(m_i, jnp.max(x, axis=-1, keepdims=True))
        pr = jnp.where(ok, jnp.exp(x - m_new), 0.0)
        scale = jnp.exp(m_i - m_new)
        l_new = scale * pair(l_scr) + jnp.sum(pr, axis=-1, keepdims=True)
        acc_new = scale * pair(acc_scr) + jnp.dot(pr.astype(BF16), vb_scr[:, ksl], preferred_element_type=F32)
        for h, rs in ((h0, slice(0, rows)), (h1, slice(rows, 2 * rows))):
            m_scr[h] = m_new[rs]
            l_scr[h] = l_new[rs]
            acc_scr[h] = acc_new[rs]

    @pl.when(p == n_steps - 1)
    def _():
        gates = jax.nn.sigmoid(gn_ref[0])
        snew = snew_ref[0]
        wnew = wnew_ref[0]
        wrows = win_ref.shape[1]
        kwb = win_ref[0, :, 0:hw].astype(BF16)
        vwb = win_ref[0, :, hw:2 * hw].astype(BF16)
        widx = lax.broadcasted_iota(jnp.int32, (1, wrows), 1)
        win_ok = widx > tok + (wrows - WINDOW)
        stacks = []
        for h in range(NSA_KV_HEADS):
            ksl = slice((h // 2) * LANES, (h // 2 + 1) * LANES)
            vsl = slice(hw + (h // 2) * LANES, hw + (h // 2 + 1) * LANES)
            qb = qh_scr[h].astype(BF16)
            qf = qb.astype(F32)
            _, l_s, acc_s = _tail_keys(qf, snew[:, ksl], snew[:, vsl], tok, m_scr[h], l_scr[h], acc_scr[h])
            o_s = acc_s / l_s
            sw = lax.dot_general(qb, kwb[:, ksl], NT_DIMS, preferred_element_type=F32) * ATTN_SCALE
            x = jnp.where(win_ok, sw, NEG_INF)
            m_w = jnp.max(x, axis=-1, keepdims=True)
            pw = jnp.where(win_ok, jnp.exp(x - m_w), 0.0)
            l_w = jnp.sum(pw, axis=-1, keepdims=True)
            acc_w = jnp.dot(pw.astype(BF16), vwb[:, ksl], preferred_element_type=F32)
            _, l_w, acc_w = _tail_keys(qf, wnew[:, ksl], wnew[:, vsl], tok, m_w, l_w, acc_w)
            o_w = acc_w / l_w
            gc = jnp.zeros((rows, 1), F32)
            gs = jnp.zeros((rows, 1), F32)
            gw = jnp.zeros((rows, 1), F32)
            for g in range(NSA_GROUP):
                head = h * NSA_GROUP + g
                gc = jnp.where(grp == g, gates[:, head:head + 1], gc)
                gs = jnp.where(grp == g, gates[:, NSA_HEADS + head:NSA_HEADS + head + 1], gs)
                gw = jnp.where(grp == g, gates[:, 2 * NSA_HEADS + head:2 * NSA_HEADS + head + 1], gw)
            stacks.append(gc * oc_scr[h] + gs * o_s + gw * o_w)
        for sidx in range(NSA_HEADS // 2):
            h = sidx // 2
            halves = []
            for g in (2 * (sidx % 2), 2 * (sidx % 2) + 1):
                part = stacks[h]
                if g > 0:
                    part = pltpu.roll(part, rows - g * nt, axis=0)
                if g % 2 != h % 2:
                    part = pltpu.roll(part, HALF, axis=1)
                halves.append(part)
            slab = jnp.where(lane < HALF, halves[0], halves[1])
            o_ref[0, :, sidx * LANES:(sidx + 1) * LANES] = slab[0:nt]


def nsa_sample(q16, gn16, kvc, sel_pages, page_table_flat, snew, wnew, win_state, *, nt, past):
    bsz, rows, _ = q16.shape
    n_pages = past // PAGE_SIZE
    assert past % SEL_BLOCK == 0 and nt <= SEL_BLOCK and rows == NSA_GROUP * nt
    assert n_pages % PAGES_PER_STEP == 0
    n_steps = n_pages // PAGES_PER_STEP
    step_keys = PAGES_PER_STEP * PAGE_SIZE
    hw = NSA_KV_HEADS * HEAD_DIM
    per_b = lambda shape, cb=0: pl.BlockSpec((1,) + shape, lambda b, p, pt: (b, 0, cb))
    hs = NSA_KV_HEADS
    return pl.pallas_call(
        functools.partial(_nsa_sample_kernel, nt=nt, past=past, n_steps=n_steps),
        grid_spec=pltpu.PrefetchScalarGridSpec(
            num_scalar_prefetch=1,
            grid=(bsz, n_steps),
            in_specs=[
                per_b((rows, Q_COLS)), per_b((rows, GN_PAD)),
                per_b((kvc.shape[1], hw), 0), per_b((kvc.shape[1], hw), 1),
                *_page_specs(PAGE_BLOCK, n_pages),
                per_b((nt, KV_ROW)), per_b((nt, KV_ROW)),
                per_b((win_state.shape[1], KV_ROW)),
            ],
            out_specs=pl.BlockSpec((1, nt, NSA_WIDTH), lambda b, p, pt: (b, 0, 0)),
            scratch_shapes=[
                pltpu.VMEM((hs, rows, LANES), F32), pltpu.VMEM((hs, rows, LANES), F32),
                pltpu.VMEM((hs, rows, LANES), F32), pltpu.VMEM((hs, rows, 1), F32),
                pltpu.VMEM((hs, rows, 1), F32), pltpu.VMEM((hs, rows, LANES), F32),
                pltpu.VMEM((step_keys, hw), BF16), pltpu.VMEM((step_keys, hw), BF16),
            ],
        ),
        out_shape=jax.ShapeDtypeStruct((bsz, nt, NSA_WIDTH), F32),
        compiler_params=_cparams("parallel", "arbitrary"),
        name="nsa_sample",
    )(page_table_flat, q16, gn16, kvc, kvc, *([sel_pages] * PAGES_PER_STEP), snew, wnew, win_state)


def _merge_kernel(za_ref, ob_ref, g0_ref, g1_ref, h_ref, wua_ref, wub_ref, wo_ref, g_ref, b_ref, o_ref, obf_ref):
    ya = jnp.dot(za_ref[...], wua_ref[...], preferred_element_type=F32)
    yb = jnp.dot(ob_ref[...], wub_ref[...], preferred_element_type=F32)
    mixed = jax.nn.sigmoid(g0_ref[...]) * ya + jax.nn.sigmoid(g1_ref[...]) * yb
    mix = jnp.dot(mixed.astype(BF16), wo_ref[...], preferred_element_type=F32)
    o = _layer_norm(DEEPNORM_ALPHA * h_ref[...] + mix, g_ref[...], b_ref[...])
    o_ref[...] = o
    obf_ref[...] = o.astype(BF16)


def merge_ln(za, ob, u, h, wua, wub, wo, g, b, *, tm):
    m, d = h.shape
    W = za.shape[1]
    assert m % tm == 0
    tile = lambda w, cb=0: pl.BlockSpec((tm, w), lambda i: (i, cb))
    const = lambda a: pl.BlockSpec(a.shape, lambda i: (0, 0), pipeline_mode=pl.Buffered(1))
    row = pl.BlockSpec((1, d), lambda i: (0, 0))
    return pl.pallas_call(
        _merge_kernel,
        grid=(m // tm,),
        in_specs=[tile(W), tile(W), tile(d, OFF_GM // d), tile(d, OFF_GM // d + 1), tile(d),
                  const(wua), const(wub), const(wo), row, row],
        out_specs=[tile(d), tile(d)],
        out_shape=[jax.ShapeDtypeStruct((m, d), F32), jax.ShapeDtypeStruct((m, d), BF16)],
        compiler_params=_cparams("parallel"),
        name="merge_ln",
    )(za, ob, u, u, h, wua, wub, wo, g.reshape(1, d), b.reshape(1, d))


TOKEN_TM = 640
PROJ_TN = 2176
MERGE_TM = 320


def kernel(x_prompt, x_sample, cache_cmp_kv, cache_sel_kv, page_table, state_win_kv, state_rwkv, state_rwkv_shift,
           ln1_g, ln1_b, ffn1_w_gu, ffn1_w_down, w_in, rwkv_mu, rwkv_w0, rwkv_w2, rwkv_a0, rwkv_a2, rwkv_g2,
           rwkv_k_k, rwkv_k_a, rwkv_r_k, rwkv_ln_w, rwkv_ln_b, w_up_a, cmp_pe, cmp_wa, cmp_wb, w_up_b, w_o,
           ln2_g, ln2_b, ffn2_w_gu, ffn2_w_down, ln3_g, ln3_b):
    bp, tp, d = x_prompt.shape
    bs, ts, _ = x_sample.shape
    assert bp == 1
    mp, ms = bp * tp, bs * ts
    n_pool = cache_cmp_kv.shape[0]
    n_pages = page_table.shape[1]
    past = n_pages * PAGE_SIZE
    kvh = (2, NSA_KV_HEADS, HEAD_DIM)
    bf = lambda a: a.astype(BF16)
    f32 = lambda a: a.astype(F32)

    x = jnp.concatenate([f32(x_prompt).reshape(mp, d), f32(x_sample).reshape(ms, d)], axis=0)
    h1, h1_bf = ffn_ln(x, bf(ffn1_w_gu), bf(ffn1_w_down), f32(ln1_g), f32(ln1_b), tm=TOKEN_TM)
    u = matmul(h1_bf, bf(pad_in_cols(w_in)), tm=TOKEN_TM, tn=PROJ_TN)
    u_s = u[mp:].reshape(bs, ts, IN_PAD)
    kv_p = u[:mp, OFF_KV:OFF_KV + KV_COLS]
    cmp_p, sel_p, win_p = kv_p[:, 0:KV_ROW], kv_p[:, KV_ROW:2 * KV_ROW], kv_p[:, 2 * KV_ROW:3 * KV_ROW]
    cmp_s = u_s[:, :, OFF_KV:OFF_KV + KV_ROW]
    sel_s = u_s[:, :, OFF_KV + KV_ROW:OFF_KV + 2 * KV_ROW]
    win_s = u_s[:, :, OFF_KV + 2 * KV_ROW:OFF_KV + 3 * KV_ROW]

    rc = rwkv_consts(rwkv_mu, rwkv_w0, rwkv_w2, rwkv_a0, rwkv_a2, rwkv_g2, rwkv_k_k, rwkv_k_a)
    za_p, s_p = rwkv_time_mix(u, 0, mp, bp, tp, None, None, rc, rwkv_r_k, rwkv_ln_w, rwkv_ln_b, prompt=True)
    za_s, s_s = rwkv_time_mix(u, mp, ms, bs, ts, state_rwkv_shift, state_rwkv, rc, rwkv_r_k, rwkv_ln_w, rwkv_ln_b,
                              prompt=False)

    cc = cmp_consts(cmp_pe, cmp_wa, cmp_wb)
    as_pages = lambda a: f32(a).reshape(-1, PAGE_SIZE, KV_ROW)
    kvc_p = compress(as_pages(cmp_p), jnp.arange(mp // PAGE_SIZE, dtype=jnp.int32), 1, mp // PAGE_SIZE, cc)[0]
    o_p = nsa_prompt(u, mp, kvc_p, bf(sel_p), bf(win_p))
    pt_flat = page_table.reshape(-1).astype(jnp.int32)
    kvc_s = compress(as_pages(cache_cmp_kv), pt_flat, bs, n_pages, cc)
    q16 = jnp.tile(u_s[:, :, OFF_Q:OFF_Q + Q_COLS], (1, NSA_GROUP, 1))
    gn16 = jnp.tile(u_s[:, :, OFF_GN:OFF_GN + GN_PAD], (1, NSA_GROUP, 1))
    o_s = nsa_sample(q16, gn16, kvc_s, as_pages(cache_sel_kv), pt_flat, sel_s, win_s,
                     f32(state_win_kv).reshape(bs, -1, KV_ROW), nt=ts, past=past)

    za = jnp.concatenate([za_p, za_s], axis=0)
    ob = jnp.concatenate([o_p, bf(o_s.reshape(ms, NSA_WIDTH))], axis=0)
    h2, _ = merge_ln(za, ob, u, h1, bf(w_up_a), bf(w_up_b), bf(w_o), f32(ln2_g), f32(ln2_b), tm=MERGE_TM)
    y, _ = ffn_ln(h2, bf(ffn2_w_gu), bf(ffn2_w_down), f32(ln3_g), f32(ln3_b), tm=TOKEN_TM)

    wb = min(WINDOW, tp)
    wkeep = min(WINDOW, state_win_kv.shape[1] + ts)
    win_all = jnp.concatenate([state_win_kv, win_s.reshape(bs, ts, *kvh).astype(state_win_kv.dtype)], axis=1)
    cd, sd, wd = cache_cmp_kv.dtype, cache_sel_kv.dtype, state_win_kv.dtype
    rd, hd = state_rwkv.dtype, state_rwkv_shift.dtype
    return (
        y[:mp].reshape(bp, tp, d).astype(x_prompt.dtype),
        y[mp:].reshape(bs, ts, d).astype(x_sample.dtype),
        cmp_p.reshape(bp, tp, *kvh).astype(cd),
        sel_p.reshape(bp, tp, *kvh).astype(sd),
        win_p[tp - wb:].reshape(bp, wb, *kvh).astype(wd),
        s_p.astype(rd),
        u[mp - 1:mp, OFF_UA:OFF_UA + RWKV_COLS].astype(hd),
        cmp_s.reshape(bs, ts, *kvh).astype(cd),
        sel_s.reshape(bs, ts, *kvh).astype(sd),
        win_all[:, win_all.shape[1] - wkeep:].astype(wd),
        s_s.astype(rd),
        u_s[:, ts - 1, OFF_UA:OFF_UA + RWKV_COLS].astype(hd),
    )
```

```python
import functools
import math

import jax
import jax.numpy as jnp
from jax import lax
from jax.experimental import pallas as pl
from jax.experimental.pallas import tpu as pltpu

F32 = jnp.float32
BF16 = jnp.bfloat16

D_MODEL = 2048
RWKV_WIDTH = D_MODEL // 2
RWKV_HEAD_SIZE = 64
RWKV_HEADS = RWKV_WIDTH // RWKV_HEAD_SIZE
DECAY_LORA = 64
AAA_LORA = 64
GATE_LORA = 160
RWKV_GN_EPS = 64e-5
NSA_WIDTH = D_MODEL // 2
HEAD_DIM = 64
NSA_HEADS = NSA_WIDTH // HEAD_DIM
NSA_KV_HEADS = 4
NSA_GROUP = NSA_HEADS // NSA_KV_HEADS
CMP_STRIDE = 16
CMP_BLOCK = 2 * CMP_STRIDE
CMP_RANK = 4
SEL_BLOCK = 64
N_SEL = 16
WINDOW = 512
Q_BLOCK = 128
PAGE_SIZE = 128
ATTN_SCALE = HEAD_DIM ** -0.5
FFN_HIDDEN = 256 * math.ceil(8 * D_MODEL / 3 / 256)
DEPTH = 1
DEEPNORM_ALPHA = (2 * DEPTH) ** 0.25
LN_EPS = 1e-5
NEG_INF = -1e30
FORCE = 1e9
RWKV_COLS = 3 * RWKV_WIDTH + DECAY_LORA + AAA_LORA + GATE_LORA
Q_COLS = NSA_HEADS * HEAD_DIM
KV_COLS = 3 * 2 * NSA_KV_HEADS * HEAD_DIM
NSA_GATE_COLS = 3 * NSA_HEADS
MERGE_GATE_COLS = 2 * D_MODEL

LANES = 128
SUBLANES = 8
VMEM_LIMIT_BYTES = 56 * 1024 * 1024

UA_PAD = 3584
GN_PAD = 128
OFF_GM = 0
OFF_Q = OFF_GM + MERGE_GATE_COLS
OFF_KV = OFF_Q + Q_COLS
OFF_UA = 2 * UA_PAD
OFF_GN = OFF_UA + UA_PAD
IN_PAD = OFF_GN + GN_PAD


def pad_in_cols(w):
    o = 0
    ua = w[..., o:o + RWKV_COLS]; o += RWKV_COLS
    q = w[..., o:o + Q_COLS]; o += Q_COLS
    kv = w[..., o:o + KV_COLS]; o += KV_COLS
    gn = w[..., o:o + NSA_GATE_COLS]; o += NSA_GATE_COLS
    gm = w[..., o:o + MERGE_GATE_COLS]
    z = lambda n: jnp.zeros(w.shape[:-1] + (n,), w.dtype)
    return jnp.concatenate([gm, q, kv, z(OFF_UA - (OFF_KV + KV_COLS)), ua, z(UA_PAD - RWKV_COLS), gn,
                            z(GN_PAD - NSA_GATE_COLS)], axis=-1)


def _cparams(*sem):
    return pltpu.CompilerParams(dimension_semantics=sem, vmem_limit_bytes=VMEM_LIMIT_BYTES)


def _layer_norm(y, g, b):
    mu = jnp.mean(y, axis=-1, keepdims=True)
    d = y - mu
    var = jnp.mean(d * d, axis=-1, keepdims=True)
    return d * lax.rsqrt(var + LN_EPS) * g + b


def _ffn_ln_kernel(x_ref, wg_ref, wu_ref, wd_ref, g_ref, b_ref, o_ref, ob_ref, xb_ref, acc_ref, *, n_chunks):
    j = pl.program_id(1)

    @pl.when(j == 0)
    def _():
        xb_ref[...] = x_ref[...].astype(BF16)
        acc_ref[...] = jnp.zeros_like(acc_ref)

    xb = xb_ref[...]
    gate = jnp.dot(xb, wg_ref[...], preferred_element_type=F32)
    up = jnp.dot(xb, wu_ref[...], preferred_element_type=F32)
    act = (gate * jax.nn.sigmoid(gate)) * up
    acc_ref[...] += jnp.dot(act.astype(BF16), wd_ref[...], preferred_element_type=F32)

    @pl.when(j == n_chunks - 1)
    def _():
        y = DEEPNORM_ALPHA * x_ref[...] + 0.5 * acc_ref[...]
        o = _layer_norm(y, g_ref[...], b_ref[...])
        o_ref[...] = o
        ob_ref[...] = o.astype(BF16)


def ffn_ln(x, w_gu_bf, w_down_bf, g, b, *, tm, tf=512):
    m, d = x.shape
    f = w_down_bf.shape[0]
    n_chunks = f // tf
    assert m % tm == 0 and f % tf == 0
    return pl.pallas_call(
        functools.partial(_ffn_ln_kernel, n_chunks=n_chunks),
        grid=(m // tm, n_chunks),
        in_specs=[
            pl.BlockSpec((tm, d), lambda i, j: (i, 0)),
            pl.BlockSpec((d, tf), lambda i, j: (0, j)),
            pl.BlockSpec((d, tf), lambda i, j: (0, j + n_chunks)),
            pl.BlockSpec((tf, d), lambda i, j: (j, 0)),
            pl.BlockSpec((1, d), lambda i, j: (0, 0)),
            pl.BlockSpec((1, d), lambda i, j: (0, 0)),
        ],
        out_specs=[
            pl.BlockSpec((tm, d), lambda i, j: (i, 0)),
            pl.BlockSpec((tm, d), lambda i, j: (i, 0)),
        ],
        out_shape=[jax.ShapeDtypeStruct((m, d), F32), jax.ShapeDtypeStruct((m, d), BF16)],
        scratch_shapes=[pltpu.VMEM((tm, d), BF16), pltpu.VMEM((tm, d), F32)],
        compiler_params=_cparams("parallel", "arbitrary"),
        name="ffn_ln",
    )(x, w_gu_bf, w_gu_bf, w_down_bf, g.reshape(1, d), b.reshape(1, d))


def _matmul_kernel(x_ref, w_ref, o_ref):
    o_ref[...] = jnp.dot(x_ref[...], w_ref[...], preferred_element_type=F32)


def matmul(x_bf, w_bf, *, tm, tn):
    m, k = x_bf.shape
    n = w_bf.shape[1]
    assert m % tm == 0 and n % tn == 0
    return pl.pallas_call(
        _matmul_kernel,
        grid=(m // tm, n // tn),
        in_specs=[pl.BlockSpec((tm, k), lambda i, j: (i, 0)), pl.BlockSpec((k, tn), lambda i, j: (0, j))],
        out_specs=pl.BlockSpec((tm, tn), lambda i, j: (i, j)),
        out_shape=jax.ShapeDtypeStruct((m, n), F32),
        compiler_params=_cparams("parallel", "arbitrary"),
        name="proj_matmul",
    )(x_bf, w_bf)


def _split2(x):
    hi = x.astype(BF16)
    lo = (x - hi.astype(F32)).astype(BF16)
    return hi, lo


def _split3(x):
    hi = x.astype(BF16)
    r1 = x - hi.astype(F32)
    mid = r1.astype(BF16)
    lo = (r1 - mid.astype(F32)).astype(BF16)
    return hi, mid, lo


def _head_sum(x, bo):
    outs = []
    for c in range(x.shape[1] // LANES):
        hi, mid, lo = _split3(x[:, c * LANES:(c + 1) * LANES])
        s = jnp.dot(hi, bo, preferred_element_type=F32)
        s += jnp.dot(mid, bo, preferred_element_type=F32)
        s += jnp.dot(lo, bo, preferred_element_type=F32)
        outs.append(s)
    return jnp.concatenate(outs, axis=1)


def _rwkv_prep_kernel(ua_ref, first_ref, mu_ref, w0_ref, a0_ref, kk_ref, ka_ref, w2_ref, a2_ref, g2_ref, bo_ref,
                      r_out, lw_out, kx_out, v_out, kkn_out, b_out, g_out, *, period):
    tm = ua_ref.shape[0]
    W = RWKV_WIDTH
    uf = ua_ref[...]
    rolled = pltpu.roll(uf, 1, axis=0)
    row = lax.broadcasted_iota(jnp.int32, (tm, 1), 0)
    if period >= tm:
        first = jnp.broadcast_to(first_ref[0:1, :], uf.shape)
        is_first = row == 0
    else:
        first = first_ref[...]
        is_first = (row % period) == 0
    prev = jnp.where(is_first, first, rolled)
    um = uf + (prev - uf) * mu_ref[...]
    r = um[:, 0:W]
    k = um[:, W:2 * W]
    v = um[:, 2 * W:3 * W]
    tail = um[:, 3 * W:UA_PAD]
    lw = jnp.dot(jnp.tanh(tail).astype(BF16), w2_ref[...], preferred_element_type=F32)
    z = -(w0_ref[...] + lw)
    softplus = jnp.maximum(z, 0.0) + jnp.log1p(jnp.exp(-jnp.abs(z)))
    w_log = -softplus - 0.5
    log_decay = -jnp.exp(w_log)
    a = jax.nn.sigmoid(a0_ref[...] + jnp.dot(tail.astype(BF16), a2_ref[...], preferred_element_type=F32))
    g = jnp.dot(jax.nn.sigmoid(tail).astype(BF16), g2_ref[...], preferred_element_type=F32)
    kk = k * kk_ref[...]
    n2 = _head_sum(kk * kk, bo_ref[...])
    kk = kk / jnp.maximum(jnp.sqrt(n2), 1e-12)
    kx = k * (1.0 + (a - 1.0) * ka_ref[...])
    r_out[...] = r
    lw_out[...] = log_decay
    kx_out[...] = kx
    v_out[...] = v
    kkn_out[...] = kk
    b_out[...] = kk * a
    g_out[...] = g


def rwkv_prep(u, first, consts, *, m0, m, tm, period):
    mu, w0, a0, k_k, k_a, w2p, a2p, g2p, bo = consts
    W = RWKV_WIDTH
    nt = m // tm
    b0 = m0 // tm
    assert m % tm == 0 and m0 % tm == 0
    first_rows = first.shape[0] // nt
    row_spec = lambda width: pl.BlockSpec((1, width), lambda i: (0, 0))
    full = lambda a: pl.BlockSpec(a.shape, lambda i: (0, 0))
    out_tile = pl.BlockSpec((tm, W), lambda i: (i, 0))
    return pl.pallas_call(
        functools.partial(_rwkv_prep_kernel, period=period),
        grid=(nt,),
        in_specs=[
            pl.BlockSpec((tm, UA_PAD), lambda i: (i + b0, OFF_UA // UA_PAD)),
            pl.BlockSpec((first_rows, UA_PAD), lambda i: (i, 0)),
            row_spec(UA_PAD), row_spec(W), row_spec(W), row_spec(W), row_spec(W),
            full(w2p), full(a2p), full(g2p), full(bo),
        ],
        out_specs=[out_tile] * 7,
        out_shape=[jax.ShapeDtypeStruct((m, W), F32)] * 7,
        compiler_params=_cparams("parallel"),
        name="rwkv_prep",
    )(u, first, mu, w0, a0, k_k, k_a, w2p, a2p, g2p, bo)


RWKV_CHUNK = 64
INV_BASE = 16
TN_DIMS = (((0,), (0,)), ((), ()))


def _mm3(a, b, dims=None):
    ah, al = _split2(a)
    bh, bl = _split2(b)
    if dims is None:
        dot = lambda x, y: jnp.dot(x, y, preferred_element_type=F32)
    else:
        dot = lambda x, y: lax.dot_general(x, y, dims, preferred_element_type=F32)
    return dot(ah, bh) + dot(ah, bl) + dot(al, bh)


def _mm_exact_lhs(a_bf, b):
    hi, mid, lo = _split3(b)
    out = jnp.dot(a_bf, hi, preferred_element_type=F32)
    out += jnp.dot(a_bf, mid, preferred_element_type=F32)
    out += jnp.dot(a_bf, lo, preferred_element_type=F32)
    return out


def _rwkv_chunk_kernel(r_ref, lw_ref, kx_ref, kk_ref, b_ref, v_ref, s0_ref, y_ref, sout_ref, s_scr, *, n_tb):
    tb = pl.program_id(1)
    C = RWKV_CHUNK
    R2 = 2 * C

    @pl.when(tb == 0)
    def _():
        s_scr[...] = s0_ref[0]

    ri = lax.broadcasted_iota(jnp.int32, (R2, R2), 0)
    ci = lax.broadcasted_iota(jnp.int32, (R2, R2), 1)
    same_head = (ri // C) == (ci // C)
    lower_strict = same_head & ((ri % C) > (ci % C))
    lower_incl = same_head & ((ri % C) >= (ci % C))
    eye = ri == ci
    same_base = (ri // INV_BASE) == (ci // INV_BASE)
    tri = jnp.where((lax.broadcasted_iota(jnp.int32, (C, C), 0) >= lax.broadcasted_iota(jnp.int32, (C, C), 1)),
                    1.0, 0.0).astype(BF16)
    lane = lax.broadcasted_iota(jnp.int32, (C, LANES), 1)
    left = lane < HALF
    lane_s = lax.broadcasted_iota(jnp.int32, (HALF, LANES), 1)
    left_s = lane_s < HALF

    def stack(x):
        return jnp.concatenate([jnp.where(left, x, 0.0), jnp.where(left, 0.0, x)], axis=0)

    def tile2(x):
        return jnp.concatenate([x, x], axis=0)

    slabs = range(RWKV_HEADS // 2)
    cols = [slice(p * LANES, (p + 1) * LANES) for p in slabs]
    each = lambda f, *xs: [f(*a) for a in zip(*xs)]
    r = [r_ref[0, :, cs] for cs in cols]
    lw = [lw_ref[0, :, cs] for cs in cols]
    kx = [kx_ref[0, :, cs] for cs in cols]
    kk = [kk_ref[0, :, cs] for cs in cols]
    b = [b_ref[0, :, cs] for cs in cols]
    v = [v_ref[0, :, cs] for cs in cols]
    cl = each(lambda x: _mm_exact_lhs(tri, x), lw)
    cl_end = each(lambda x: x[C - 1:C], cl)
    g_inv = each(lambda x: jnp.exp(-x), cl)
    g_end = each(lambda e, x: jnp.exp(e - x), cl_end, cl)
    kt = each(lambda k_, c_, l_: k_ * jnp.exp(c_ - l_), kk, cl, lw)
    rt = each(lambda r_, c_: r_ * jnp.exp(c_), r, cl)
    bh = each(jnp.multiply, b, g_inv)
    kh = each(jnp.multiply, kx, g_inv)
    bbar = each(jnp.multiply, b, g_end)
    kbar = each(jnp.multiply, kx, g_end)
    gram = each(lambda kt_, rt_, bh_, kh_: _mm3(jnp.concatenate([kt_, rt_], axis=0),
                                                jnp.concatenate([stack(bh_), stack(kh_)], axis=0), NT_DIMS),
                kt, rt, bh, kh)
    n_mat = each(lambda g_: jnp.where(lower_strict, tile2(g_[0:C, 0:R2]), 0.0), gram)
    ak = each(lambda g_: jnp.where(lower_strict, tile2(g_[0:C, R2:2 * R2]), 0.0), gram)
    bb = each(lambda g_: jnp.where(lower_incl, tile2(g_[C:R2, 0:R2]), 0.0), gram)
    bk = each(lambda g_: jnp.where(lower_incl, tile2(g_[C:R2, R2:2 * R2]), 0.0), gram)
    d1 = each(lambda n_: jnp.where(same_base, n_, 0.0), n_mat)
    d2 = each(_mm3, d1, d1)
    d4 = each(_mm3, d2, d2)
    d8 = each(_mm3, d4, d4)
    t_inv = each(lambda d_: jnp.where(eye, 1.0, 0.0) - d_, d1)
    for dk in (d2, d4, d8):
        t_inv = each(lambda t_, d_: t_ + _mm3(t_, d_), t_inv, dk)
    blk = INV_BASE
    while blk < C:
        off_mask = ((ri // (2 * blk)) == (ci // (2 * blk))) & ((ri // blk) != (ci // blk))
        tmp = each(lambda t_, n_: _mm3(t_, jnp.where(off_mask, n_, 0.0)), t_inv, n_mat)
        t_inv = each(lambda t_, m_: t_ - _mm3(m_, t_), t_inv, tmp)
        blk *= 2
    v_st = each(stack, v)
    kt_st = each(stack, kt)
    w_st = each(_mm3, ak, v_st)
    pq = each(lambda t_, k_, w_: _mm3(t_, jnp.concatenate([k_, w_], axis=1)), t_inv, kt_st, w_st)
    z = each(lambda bb_, bk_, pq_, v_: _mm3(
        jnp.concatenate([-bb_, bk_], axis=1),
        jnp.concatenate([pq_, jnp.concatenate([jnp.zeros_like(v_), v_], axis=1)], axis=0)), bb, bk, pq, v_st)
    ry_st = each(lambda rt_, z_: stack(rt_) + z_[:, 0:LANES], rt, z)
    bbar_st = each(stack, bbar)
    g_mat = each(lambda e_, pq_, bs_: jnp.where(eye, jnp.broadcast_to(jnp.exp(e_), (R2, LANES)), 0.0)
                 - _mm3(pq_[:, 0:LANES].T, bs_), cl_end, pq, bbar_st)
    h_bd = each(lambda v_, pq_, kb_, bs_: _mm3(jnp.concatenate([v_, pq_[:, LANES:2 * LANES]], axis=0).T,
                                               jnp.concatenate([stack(kb_), -bs_], axis=0)), v_st, pq, kbar, bbar_st)
    for p in slabs:
        s_old = s_scr[p]
        s_dup = jnp.concatenate([jnp.where(left_s, s_old, 0.0), jnp.where(left_s, 0.0, s_old)], axis=0)
        y_st = _mm3(ry_st[p], s_dup, NT_DIMS) + z[p][:, LANES:2 * LANES]
        y_ref[0, :, cols[p]] = y_st[0:C] + y_st[C:R2]
        s_scr[p] = _mm3(s_old, g_mat[p]) + (h_bd[p][0:HALF] + h_bd[p][HALF:LANES])

    @pl.when(tb == n_tb - 1)
    def _():
        sout_ref[0] = s_scr[...]


def rwkv_chunk_scan(r, lw, kx, kk, b, v, s0_slabs):
    bsz, t, W = r.shape
    n_tb = t // RWKV_CHUNK
    assert t % RWKV_CHUNK == 0
    seq = pl.BlockSpec((1, RWKV_CHUNK, W), lambda bi, ti: (bi, ti, 0))
    st = pl.BlockSpec((1, RWKV_HEADS // 2, 64, LANES), lambda bi, ti: (bi, 0, 0, 0))
    return pl.pallas_call(
        functools.partial(_rwkv_chunk_kernel, n_tb=n_tb),
        grid=(bsz, n_tb),
        in_specs=[seq] * 6 + [st],
        out_specs=[seq, st],
        out_shape=[jax.ShapeDtypeStruct((bsz, t, W), F32),
                   jax.ShapeDtypeStruct((bsz, RWKV_HEADS // 2, 64, LANES), F32)],
        scratch_shapes=[pltpu.VMEM((RWKV_HEADS // 2, 64, LANES), F32)],
        compiler_params=_cparams("parallel", "arbitrary"),
        name="rwkv_chunk_scan",
    )(r, lw, kx, kk, b, v, s0_slabs)


def _rwkv_post_kernel(y_ref, r_ref, kx_ref, v_ref, g_ref, rk_ref, lnw_ref, lnb_ref, bo_ref, o_ref):
    bo = bo_ref[...]
    y = y_ref[...]
    inv_n = 1.0 / RWKV_HEAD_SIZE
    ym = _head_sum(y, bo) * inv_n
    d = y - ym
    yv = _head_sum(d * d, bo) * inv_n
    yn = d * lax.rsqrt(yv + RWKV_GN_EPS) * lnw_ref[...] + lnb_ref[...]
    bonus = _head_sum(r_ref[...] * kx_ref[...] * rk_ref[...], bo) * v_ref[...]
    o_ref[...] = ((yn + bonus) * g_ref[...]).astype(BF16)


def rwkv_post(y, r, kx, v, g, r_k, ln_w, ln_b, bo, *, tm):
    m, W = y.shape
    tile = pl.BlockSpec((tm, W), lambda i: (i, 0))
    row = pl.BlockSpec((1, W), lambda i: (0, 0))
    return pl.pallas_call(
        _rwkv_post_kernel,
        grid=(m // tm,),
        in_specs=[tile] * 5 + [row] * 3 + [pl.BlockSpec(bo.shape, lambda i: (0, 0))],
        out_specs=tile,
        out_shape=jax.ShapeDtypeStruct((m, W), BF16),
        compiler_params=_cparams("parallel"),
        name="rwkv_post",
    )(y, r, kx, v, g, r_k, ln_w, ln_b, bo)


def rwkv_consts(mu, w0, w2, a0, a2, g2, k_k, k_a):
    W = RWKV_WIDTH
    tail = UA_PAD - 3 * W
    mu_p = jnp.zeros((1, UA_PAD), F32).at[0, :RWKV_COLS].set(mu.astype(F32))
    o_a = DECAY_LORA
    o_g = DECAY_LORA + AAA_LORA
    w2p = jnp.zeros((tail, W), BF16).at[0:o_a].set(w2.astype(BF16))
    a2p = jnp.zeros((tail, W), BF16).at[o_a:o_g].set(a2.astype(BF16))
    g2p = jnp.zeros((tail, W), BF16).at[o_g:o_g + GATE_LORA].set(g2.astype(BF16))
    half = jnp.arange(LANES) // RWKV_HEAD_SIZE
    bo = (half[:, None] == half[None, :]).astype(BF16)
    row = lambda p: p.astype(F32).reshape(1, W)
    return (mu_p, row(w0), row(a0), row(k_k), row(k_a), w2p, a2p, g2p, bo)


def _state_to_slabs(s):
    bsz = s.shape[0]
    hp = RWKV_HEADS // 2
    return s.reshape(bsz, hp, 2, 64, 64).transpose(0, 1, 3, 2, 4).reshape(bsz, hp, 64, LANES)


def _slabs_to_state(s):
    bsz = s.shape[0]
    hp = RWKV_HEADS // 2
    return s.reshape(bsz, hp, 64, 2, 64).transpose(0, 1, 3, 2, 4).reshape(bsz, RWKV_HEADS, 64, 64)


PREP_TM = 256


def rwkv_time_mix(u, m0, m, bsz, t, shift_state, s0, consts, r_k, ln_w, ln_b, *, prompt):
    W = RWKV_WIDTH
    bo = consts[-1]
    if prompt:
        tm = min(PREP_TM, m)
        nt = m // tm
        prev_rows = u[m0 + tm - 1:m0 + m - 1:tm, OFF_UA:OFF_UA + UA_PAD]
        rows = jnp.concatenate([jnp.zeros((1, UA_PAD), F32), prev_rows], axis=0)
        first = jnp.zeros((nt, SUBLANES, UA_PAD), F32).at[:, 0].set(rows).reshape(nt * SUBLANES, UA_PAD)
        period = tm
        s0_slabs = jnp.zeros((bsz, RWKV_HEADS // 2, 64, LANES), F32)
    else:
        tm = m
        sp = jnp.zeros((bsz, UA_PAD), F32).at[:, :RWKV_COLS].set(shift_state.astype(F32))
        first = jnp.repeat(sp, t, axis=0)
        period = t
        s0_slabs = _state_to_slabs(s0.astype(F32))
    r, lw, kx, v, kk, b, g = rwkv_prep(u, first, consts, m0=m0, m=m, tm=tm, period=period)
    tpad = -(-t // RWKV_CHUNK) * RWKV_CHUNK
    seq = lambda a: jnp.pad(a.reshape(bsz, t, W), ((0, 0), (0, tpad - t), (0, 0)))
    y3, s_fin = rwkv_chunk_scan(seq(r), seq(lw), seq(kx), seq(kk), seq(b), seq(v), s0_slabs)
    y = y3[:, :t].reshape(m, W)
    za = rwkv_post(y, r, kx, v, g, r_k.astype(F32).reshape(1, W), ln_w.astype(F32).reshape(1, W),
                   ln_b.astype(F32).reshape(1, W), bo, tm=tm)
    return za, _slabs_to_state(s_fin)


KV_ROW = 2 * NSA_KV_HEADS * HEAD_DIM
CHUNKS_PER_PAGE = PAGE_SIZE // CMP_STRIDE
PROJ_W = CMP_RANK * KV_ROW


PAGES_PER_STEP = 16


def _page_specs(block, n_pages):
    def spec(i):
        return pl.BlockSpec(block, lambda b, p, pt: (pt[b * n_pages + p * PAGES_PER_STEP + i],) + (0,) * (len(block) - 1))
    return [spec(i) for i in range(PAGES_PER_STEP)]


PAGE_BLOCK = (1, PAGE_SIZE, KV_ROW)


def _cmp_proj_kernel(pt_ref, *refs):
    del pt_ref
    x_refs = refs[:PAGES_PER_STEP]
    w_ref, f_ref, s_ref = refs[PAGES_PER_STEP:]
    half = KV_ROW // 2
    n_out = CMP_RANK * CHUNKS_PER_PAGE
    for i, x_ref in enumerate(x_refs):
        rows = slice(i * CHUNKS_PER_PAGE, (i + 1) * CHUNKS_PER_PAGE)
        for s in range(2):
            out = jnp.dot(w_ref[s], x_ref[0, :, s * half:(s + 1) * half].astype(BF16), preferred_element_type=F32)
            for r in range(CMP_RANK):
                cols = slice(r * KV_ROW + s * half, r * KV_ROW + (s + 1) * half)
                f_ref[0, rows, cols] = out[r * CHUNKS_PER_PAGE:(r + 1) * CHUNKS_PER_PAGE]
                s_ref[0, rows, cols] = out[n_out + r * CHUNKS_PER_PAGE:n_out + (r + 1) * CHUNKS_PER_PAGE]


def cmp_project(pages, page_table_flat, bsz, n_pages, w_proj):
    assert n_pages % PAGES_PER_STEP == 0
    n_ch = n_pages * CHUNKS_PER_PAGE
    step_ch = PAGES_PER_STEP * CHUNKS_PER_PAGE
    out = jax.ShapeDtypeStruct((bsz, n_ch, PROJ_W), F32)
    o_spec = pl.BlockSpec((1, step_ch, PROJ_W), lambda b, p, pt: (b, p, 0))
    w_spec = pl.BlockSpec(w_proj.shape, lambda b, p, pt: (0, 0, 0))
    return pl.pallas_call(
        _cmp_proj_kernel,
        grid_spec=pltpu.PrefetchScalarGridSpec(
            num_scalar_prefetch=1,
            grid=(bsz, n_pages // PAGES_PER_STEP),
            in_specs=_page_specs(PAGE_BLOCK, n_pages) + [w_spec],
            out_specs=[o_spec, o_spec],
        ),
        out_shape=[out, out],
        compiler_params=_cparams("parallel", "arbitrary"),
        name="cmp_project",
    )(page_table_flat, *([pages] * PAGES_PER_STEP), w_proj)


def _cmp_mix_kernel(f_ref, s_ref, waf_ref, was_ref, pex_ref, wb_ref, o_ref):
    n_ch = f_ref.shape[1]
    acc = jnp.zeros((n_ch, KV_ROW), F32)
    for r in range(CMP_RANK):
        rs = slice(r * CMP_STRIDE, (r + 1) * CMP_STRIDE)
        pe = jnp.sum(waf_ref[rs, :] * pex_ref[0:CMP_STRIDE, :], axis=0, keepdims=True)
        pe += jnp.sum(was_ref[rs, :] * pex_ref[CMP_STRIDE:CMP_BLOCK, :], axis=0, keepdims=True)
        cs = slice(r * KV_ROW, (r + 1) * KV_ROW)
        nxt = pltpu.roll(s_ref[0, :, cs], n_ch - 1, axis=0)
        hid = jax.nn.gelu(f_ref[0, :, cs] + nxt + pe)
        acc += jnp.dot(hid.astype(BF16), wb_ref[r], preferred_element_type=F32)
    o_ref[0] = acc.astype(BF16)


def cmp_mix(f, s, waf, was, pex, wb_bd):
    bsz, n_ch, _ = f.shape
    io = pl.BlockSpec((1, n_ch, PROJ_W), lambda b: (b, 0, 0))
    full2 = lambda a: pl.BlockSpec(a.shape, lambda b: (0, 0))
    return pl.pallas_call(
        _cmp_mix_kernel,
        grid=(bsz,),
        in_specs=[io, io, full2(waf), full2(was), full2(pex), pl.BlockSpec(wb_bd.shape, lambda b: (0, 0, 0))],
        out_specs=pl.BlockSpec((1, n_ch, KV_ROW), lambda b: (b, 0, 0)),
        out_shape=jax.ShapeDtypeStruct((bsz, n_ch, KV_ROW), BF16),
        compiler_params=_cparams("parallel"),
        name="cmp_mix",
    )(f, s, waf, was, pex, wb_bd)


def cmp_consts(cmp_pe, cmp_wa, cmp_wb):
    hd = NSA_KV_HEADS * HEAD_DIM
    wa_cols = jnp.repeat(cmp_wa.astype(F32).transpose(2, 1, 0), hd, axis=2)
    waf = wa_cols[:, :CMP_STRIDE].reshape(CMP_RANK * CMP_STRIDE, KV_ROW)
    was = wa_cols[:, CMP_STRIDE:].reshape(CMP_RANK * CMP_STRIDE, KV_ROW)
    pex = jnp.broadcast_to(cmp_pe.astype(F32).transpose(1, 0, 2)[:, :, None, :],
                           (CMP_BLOCK, 2, NSA_KV_HEADS, HEAD_DIM)).reshape(CMP_BLOCK, KV_ROW)
    eye = jnp.eye(NSA_KV_HEADS, dtype=F32)
    wb = cmp_wb.astype(F32)
    bd = jnp.einsum('srde,hg,st->rshdtge', wb, eye, jnp.eye(2, dtype=F32)).reshape(CMP_RANK, KV_ROW, KV_ROW)
    wa4 = cmp_wa.astype(F32).reshape(2, 2, CMP_STRIDE, CMP_RANK)
    w_proj = jnp.einsum('sflr,nm->sfrnml', wa4, jnp.eye(CHUNKS_PER_PAGE, dtype=F32)).reshape(
        2, 2 * CMP_RANK * CHUNKS_PER_PAGE, PAGE_SIZE)
    return waf, was, pex, bd.astype(BF16), w_proj.astype(BF16)


def compress(pages, page_table_flat, bsz, n_pages, consts):
    waf, was, pex, wb_bd, w_proj = consts
    f, s = cmp_project(pages, page_table_flat, bsz, n_pages, w_proj)
    kvc = cmp_mix(f, s, waf, was, pex, wb_bd)
    ratio = SEL_BLOCK // CMP_STRIDE
    n_ch = LANES * ratio
    kvc = jnp.pad(kvc, ((0, 0), (0, n_ch - kvc.shape[1]), (0, 0)))
    return kvc.reshape(bsz, LANES, ratio, KV_ROW).transpose(0, 2, 1, 3).reshape(bsz, n_ch, KV_ROW)


NT_DIMS = (((1,), (1,)), ((), ()))
HALF = LANES // 2
KEY_TILE = 512
WIN_SPAN = WINDOW + Q_BLOCK
TOPK_UNROLL = 8


def _masked_softmax_rows(s, ok):
    x = jnp.where(ok, s, NEG_INF)
    m = jnp.max(x, axis=-1, keepdims=True)
    e = jnp.exp(x - m)
    l = jnp.sum(e, axis=-1, keepdims=True)
    return jnp.where(ok, e / l, 0.0)


def _top_blocks(score, n_keep):
    lane = lax.broadcasted_iota(jnp.int32, score.shape, 1)

    def body(d, cnt):
        rolled = pltpu.roll(score, d, axis=1)
        beats = (rolled > score) | ((rolled == score) & (lane >= d))
        return cnt + jnp.where(beats, 1.0, 0.0)

    cnt = lax.fori_loop(1, LANES, body, jnp.zeros(score.shape, F32), unroll=TOPK_UNROLL)
    return cnt < n_keep


def _rank_unselected(st_ref, cnt_ref, n_keep, n_live):
    n = st_ref.shape[1]
    sub = lax.broadcasted_iota(jnp.int32, (SUBLANES, LANES), 0)
    n_tiles = LANES // SUBLANES
    cnt_ref[...] = jnp.zeros(cnt_ref.shape, F32)
    for jg in range(n_tiles):

        @pl.when(jg * SUBLANES < n_live)
        def _():
            for c in range(n // LANES):
                cs = slice(c * LANES, (c + 1) * LANES)
                rows = [st_ref[j:j + 1, cs] for j in range(jg * SUBLANES, (jg + 1) * SUBLANES)]
                for t in range(n_tiles):
                    ts_ = slice(t * SUBLANES, (t + 1) * SUBLANES)
                    si = st_ref[ts_, cs]
                    cnt = cnt_ref[ts_, cs]
                    for jj, sj in enumerate(rows):
                        if t > jg:
                            beats = sj >= si
                        elif t < jg:
                            beats = sj > si
                        else:
                            beats = (sj > si) | ((sj == si) & (sub > jj))
                        cnt = cnt + jnp.where(beats, 1.0, 0.0)
                    cnt_ref[ts_, cs] = cnt
    return jnp.where(cnt_ref[...] < n_keep, 0.0, 1.0).astype(BF16)


def _col_softmax(x):
    m = jnp.max(x, axis=0, keepdims=True)
    e = jnp.exp(x - m)
    l = jnp.sum(e, axis=0, keepdims=True)
    return e * (1.0 / l), m, l


def _nsa_prompt_kernel(q_ref, gn_ref, kc_ref, vc_ref, ks_ref, vs_ref, kw_ref, vw_ref, o_ref, st_scr, cnt_scr):
    i = pl.program_id(0)
    start = i * Q_BLOCK
    assert math.log2(ATTN_SCALE).is_integer()
    q = q_ref[...] * ATTN_SCALE
    gates_t = jax.nn.sigmoid(gn_ref[...]).T
    pos = start + lax.broadcasted_iota(jnp.int32, (1, Q_BLOCK), 1)
    n_cmp = kc_ref.shape[0]
    ratio = SEL_BLOCK // CMP_STRIDE
    crow = lax.broadcasted_iota(jnp.int32, (n_cmp, 1), 0)
    cmp_idx = (crow % LANES) * ratio + crow // LANES
    cmp_ok = (cmp_idx * CMP_STRIDE + (CMP_BLOCK - 1)) <= pos
    cmp_bias = jnp.where(cmp_ok, 0.0, NEG_INF)
    blk = lax.broadcasted_iota(jnp.int32, (LANES, 1), 0)
    cur = pos // SEL_BLOCK
    forced = (blk == 0) | (blk == cur) | (blk == cur - 1)
    future = blk * SEL_BLOCK > pos
    key_sub = lax.broadcasted_iota(jnp.int32, (KEY_TILE, 1), 0)
    blk_lane = lax.broadcasted_iota(jnp.int32, (1, LANES), 1)
    lane_q = lax.broadcasted_iota(jnp.int32, (Q_BLOCK, LANES), 1)
    n_tiles = start // KEY_TILE + 1
    wb0 = jnp.maximum(start - WINDOW, 0) // Q_BLOCK
    n_wblk = WIN_SPAN // Q_BLOCK

    heads = range(NSA_KV_HEADS)
    groups = range(NSA_GROUP)
    ksls = [slice((h // 2) * LANES, (h // 2 + 1) * LANES) for h in heads]
    keeps = [(lane_q // HALF) == (h % 2) for h in heads]
    qs = []
    for h in heads:
        row = []
        for g in groups:
            head = h * NSA_GROUP + g
            slab = q[:, (head // 2) * LANES:(head // 2 + 1) * LANES]
            if head % 2 != h % 2:
                slab = pltpu.roll(slab, HALF, axis=1)
            row.append(jnp.where(keeps[h], slab, 0.0).astype(BF16))
        qs.append(row)

    pairs = [(h, g) for h in heads for g in groups]
    cs_ = [lax.dot_general(kc_ref[:, ksls[h]], qs[h][g], NT_DIMS, preferred_element_type=F32) + cmp_bias
           for h, g in pairs]
    pcs = [jnp.where(cmp_ok, _col_softmax(x)[0], 0.0) for x in cs_]
    o_c = [lax.dot_general(vc_ref[:, ksls[h]], pcs[i].astype(BF16), TN_DIMS, preferred_element_type=F32)
           for i, (h, g) in enumerate(pairs)]
    for h in heads:
        imp = pcs[h * NSA_GROUP]
        for g in range(1, NSA_GROUP):
            imp = imp + pcs[h * NSA_GROUP + g]
        imp_blk = imp[0:LANES]
        for j in range(1, ratio):
            imp_blk = imp_blk + imp[j * LANES:(j + 1) * LANES]
        st_scr[:, h * Q_BLOCK:(h + 1) * Q_BLOCK] = jnp.where(future, -FORCE, jnp.where(forced, FORCE, imp_blk))
    n_live = (start + Q_BLOCK - 1) // SEL_BLOCK + 1
    unsel = _rank_unselected(st_scr, cnt_scr, N_SEL, n_live)

    def sel_tile(kt, carry, causal):
        k0 = kt * KEY_TILE
        expand = jnp.where((k0 + key_sub) // SEL_BLOCK == blk_lane, NEG_INF, 0.0).astype(BF16)
        biases = [jnp.dot(expand, unsel[:, h * Q_BLOCK:(h + 1) * Q_BLOCK], preferred_element_type=F32) for h in heads]
        if causal:
            visible = (k0 + key_sub) <= pos
            biases = [jnp.where(visible, b_, NEG_INF) for b_ in biases]
        biases2 = [jnp.concatenate([b_, b_], axis=1) for b_ in biases]
        xs = [lax.dot_general(ks_ref[kt, :, ksls[h]], q2[h][gp], NT_DIMS, preferred_element_type=F32) + biases2[h]
              for h, gp in duos]
        m_new = [jnp.maximum(carry[i][0], jnp.max(x, axis=0, keepdims=True)) for i, x in enumerate(xs)]
        ps = [jnp.exp(x - m) for x, m in zip(xs, m_new)]
        scs = [jnp.exp(carry[i][0] - m) for i, m in enumerate(m_new)]
        l_new = [sc * carry[i][1] + jnp.sum(p, axis=0, keepdims=True) for i, (sc, p) in enumerate(zip(scs, ps))]
        pvs = [lax.dot_general(vs_ref[kt, :, ksls[h]], ps[i].astype(BF16), TN_DIMS, preferred_element_type=F32)
               for i, (h, gp) in enumerate(duos)]
        return tuple((m_new[i], l_new[i], scs[i] * carry[i][2] + pvs[i]) for i in range(len(duos)))

    duos = [(h, gp) for h in heads for gp in range(NSA_GROUP // 2)]
    q2 = [[jnp.concatenate([qs[h][2 * gp], qs[h][2 * gp + 1]], axis=0) for gp in range(NSA_GROUP // 2)] for h in heads]
    init = tuple((jnp.full((1, 2 * Q_BLOCK), NEG_INF, F32), jnp.zeros((1, 2 * Q_BLOCK), F32),
                  jnp.zeros((LANES, 2 * Q_BLOCK), F32)) for _ in duos)
    carry2 = lax.fori_loop(0, n_tiles - 1, functools.partial(sel_tile, causal=False), init)
    carry2 = sel_tile(n_tiles - 1, carry2, True)
    carry = []
    for _, l2, acc2 in carry2:
        for half in range(2):
            hs_ = slice(half * Q_BLOCK, (half + 1) * Q_BLOCK)
            carry.append((None, l2[:, hs_], acc2[:, hs_]))

    kpos = wb0 * Q_BLOCK + lax.broadcasted_iota(jnp.int32, (WIN_SPAN, 1), 0)
    delta = pos - kpos
    win_bias = jnp.where((delta >= 0) & (delta < WINDOW), 0.0, NEG_INF)
    k_ws = [jnp.concatenate([kw_ref[wb0 + j, :, ksls[h]] for j in range(n_wblk)], axis=0) for h in heads]
    ws_ = [lax.dot_general(k_ws[h], qs[h][g], NT_DIMS, preferred_element_type=F32) + win_bias for h, g in pairs]
    pws = [_col_softmax(x)[0].astype(BF16) for x in ws_]
    o_w = []
    for i, (h, g) in enumerate(pairs):
        acc = lax.dot_general(vw_ref[wb0, :, ksls[h]], pws[i][0:Q_BLOCK], TN_DIMS, preferred_element_type=F32)
        for j in range(1, n_wblk):
            acc += lax.dot_general(vw_ref[wb0 + j, :, ksls[h]], pws[i][j * Q_BLOCK:(j + 1) * Q_BLOCK], TN_DIMS,
                                   preferred_element_type=F32)
        o_w.append(acc)
    pieces = []
    for head, (h, g) in enumerate(pairs):
        _, l_s, acc_s = carry[head]
        o_s = acc_s * (1.0 / l_s)
        mix = (gates_t[head:head + 1] * o_c[head] + gates_t[NSA_HEADS + head:NSA_HEADS + head + 1] * o_s
               + gates_t[2 * NSA_HEADS + head:2 * NSA_HEADS + head + 1] * o_w[head])
        pieces.append(mix[(h % 2) * HALF:(h % 2 + 1) * HALF])
    o_t = jnp.concatenate(pieces, axis=0)
    o_ref[...] = o_t.T.astype(BF16)


def nsa_prompt(u, t, kvc, sel_kv_bf, win_kv_bf):
    n_qb = t // Q_BLOCK
    hw = NSA_KV_HEADS * HEAD_DIM
    tiles = lambda a, rows: a.reshape(t // rows, rows, hw)
    operands = [
        kvc[:, :hw], kvc[:, hw:],
        tiles(sel_kv_bf[:, :hw], KEY_TILE), tiles(sel_kv_bf[:, hw:], KEY_TILE),
        tiles(win_kv_bf[:, :hw], Q_BLOCK), tiles(win_kv_bf[:, hw:], Q_BLOCK),
    ]
    whole = lambda a: pl.BlockSpec(a.shape, lambda i: (0,) * a.ndim)
    return pl.pallas_call(
        _nsa_prompt_kernel,
        grid=(n_qb,),
        in_specs=[
            pl.BlockSpec((Q_BLOCK, Q_COLS), lambda i: (i, OFF_Q // Q_COLS)),
            pl.BlockSpec((Q_BLOCK, GN_PAD), lambda i: (i, OFF_GN // GN_PAD)),
        ] + [whole(a) for a in operands],
        out_specs=pl.BlockSpec((Q_BLOCK, NSA_WIDTH), lambda i: (i, 0)),
        out_shape=jax.ShapeDtypeStruct((t, NSA_WIDTH), BF16),
        scratch_shapes=[pltpu.VMEM((LANES, NSA_KV_HEADS * Q_BLOCK), F32)] * 2,
        compiler_params=_cparams("arbitrary"),
        name="nsa_prompt",
    )(u, u, *operands)


def _tail_keys(qf, k_new, v_new, tok, m, l, acc):
    nt = k_new.shape[0]
    kn = k_new.astype(BF16).astype(F32)
    vn = v_new.astype(BF16).astype(F32)
    s = [jnp.sum(qf * kn[j:j + 1], axis=1, keepdims=True) * ATTN_SCALE for j in range(nt)]
    ok = [tok >= j for j in range(nt)]
    m_new = m
    for j in range(nt):
        m_new = jnp.maximum(m_new, jnp.where(ok[j], s[j], NEG_INF))
    scale = jnp.exp(m - m_new)
    l = scale * l
    acc = scale * acc
    for j in range(nt):
        p = jnp.where(ok[j], jnp.exp(s[j] - m_new), 0.0)
        l = l + p
        acc = acc + p * vn[j:j + 1]
    return m_new, l, acc


def _nsa_sample_kernel(pt_ref, q_ref, gn_ref, kc_ref, vc_ref, *refs, nt, past, n_steps):
    del pt_ref
    page_refs = refs[:PAGES_PER_STEP]
    (snew_ref, wnew_ref, win_ref, o_ref,
     qh_scr, sel_scr, oc_scr, m_scr, l_scr, acc_scr, kb_scr, vb_scr) = refs[PAGES_PER_STEP:]
    p = pl.program_id(1)
    rows = NSA_GROUP * nt
    hw = NSA_KV_HEADS * HEAD_DIM
    row1 = lax.broadcasted_iota(jnp.int32, (rows, 1), 0)
    tok = row1 % nt
    grp = row1 // nt
    pos = past + tok
    lane = lax.broadcasted_iota(jnp.int32, (rows, LANES), 1)
    blk = lax.broadcasted_iota(jnp.int32, (1, LANES), 1)
    ratio = SEL_BLOCK // CMP_STRIDE

    @pl.when(p == 0)
    def _():
        q16 = q_ref[0]
        n_cmp = kc_ref.shape[1]
        col = lax.broadcasted_iota(jnp.int32, (1, n_cmp), 1)
        cmp_idx = (col % LANES) * ratio + col // LANES
        cmp_ok = (cmp_idx * CMP_STRIDE + (CMP_BLOCK - 1)) <= pos
        cur = pos // SEL_BLOCK
        forced = (blk == 0) | (blk == cur) | (blk == cur - 1)
        future = blk * SEL_BLOCK > pos
        for h in range(NSA_KV_HEADS):
            khalf = h % 2
            ksl = slice((h // 2) * LANES, (h // 2 + 1) * LANES)
            keep = (lane // HALF) == khalf
            qh = jnp.zeros((rows, LANES), F32)
            for g in range(NSA_GROUP):
                head = h * NSA_GROUP + g
                slab = q16[:, (head // 2) * LANES:(head // 2 + 1) * LANES]
                if head % 2 != khalf:
                    slab = pltpu.roll(slab, HALF, axis=1)
                qh = jnp.where((grp == g) & keep, slab, qh)
            qh_scr[h] = qh
            sc = lax.dot_general(qh.astype(BF16), kc_ref[0, :, ksl], NT_DIMS, preferred_element_type=F32) * ATTN_SCALE
            pc = _masked_softmax_rows(sc, cmp_ok)
            oc_scr[h] = jnp.dot(pc.astype(BF16), vc_ref[0, :, ksl], preferred_element_type=F32)
            imp = pc
            for g in range(1, NSA_GROUP):
                imp = imp + pltpu.roll(pc, g * nt, axis=0)
            imp_blk = imp[:, 0:LANES]
            for j in range(1, ratio):
                imp_blk = imp_blk + imp[:, j * LANES:(j + 1) * LANES]
            score = jnp.where(future, -FORCE, jnp.where(forced, FORCE, imp_blk))
            sel_scr[h] = jnp.where(_top_blocks(score, N_SEL - 1), 1.0, 0.0)
            m_scr[h] = jnp.full((rows, 1), NEG_INF, F32)
            l_scr[h] = jnp.zeros((rows, 1), F32)
            acc_scr[h] = jnp.zeros((rows, LANES), F32)

    step_keys = PAGES_PER_STEP * PAGE_SIZE
    for i, page_ref in enumerate(page_refs):
        ks = slice(i * PAGE_SIZE, (i + 1) * PAGE_SIZE)
        kb_scr[ks, :] = page_ref[0, :, 0:hw].astype(BF16)
        vb_scr[ks, :] = page_ref[0, :, hw:2 * hw].astype(BF16)
    blk_row = lax.broadcasted_iota(jnp.int32, (LANES, 1), 0)
    key_lane = lax.broadcasted_iota(jnp.int32, (1, step_keys), 1)
    expand = jnp.where(blk_row == (p * step_keys + key_lane) // SEL_BLOCK, 1.0, 0.0).astype(BF16)
    for slab in range(NSA_KV_HEADS // 2):
        h0, h1 = 2 * slab, 2 * slab + 1
        ksl = slice(slab * LANES, (slab + 1) * LANES)
        pair = lambda ref: jnp.concatenate([ref[h0], ref[h1]], axis=0)
        s = lax.dot_general(pair(qh_scr).astype(BF16), kb_scr[:, ksl], NT_DIMS,
                            preferred_element_type=F32) * ATTN_SCALE
        ok = jnp.dot(pair(sel_scr).astype(BF16), expand, preferred_element_type=F32) > 0.5
        x = jnp.where(ok, s, NEG_INF)
        m_i = pair(m_scr)
        m_new = jnp.maximum(m_i, jnp.max(x, axis=-1, keepdims=True))
        pr = jnp.where(ok, jnp.exp(x - m_new), 0.0)
        scale = jnp.exp(m_i - m_new)
        l_new = scale * pair(l_scr) + jnp.sum(pr, axis=-1, keepdims=True)
        acc_new = scale * pair(acc_scr) + jnp.dot(pr.astype(BF16), vb_scr[:, ksl], preferred_element_type=F32)
        for h, rs in ((h0, slice(0, rows)), (h1, slice(rows, 2 * rows))):
            m_scr[h] = m_new[rs]
            l_scr[h] = l_new[rs]
            acc_scr[h] = acc_new[rs]

    @pl.when(p == n_steps - 1)
    def _():
        gates = jax.nn.sigmoid(gn_ref[0])
        snew = snew_ref[0]
        wnew = wnew_ref[0]
        wrows = win_ref.shape[1]
        kwb = win_ref[0, :, 0:hw].astype(BF16)
        vwb = win_ref[0, :, hw:2 * hw].astype(BF16)
        widx = lax.broadcasted_iota(jnp.int32, (1, wrows), 1)
        win_ok = widx > tok + (wrows - WINDOW)
        stacks = []
        for h in range(NSA_KV_HEADS):
            ksl = slice((h // 2) * LANES, (h // 2 + 1) * LANES)
            vsl = slice(hw + (h // 2) * LANES, hw + (h // 2 + 1) * LANES)
            qb = qh_scr[h].astype(BF16)
            qf = qb.astype(F32)
            _, l_s, acc_s = _tail_keys(qf, snew[:, ksl], snew[:, vsl], tok, m_scr[h], l_scr[h], acc_scr[h])
            o_s = acc_s / l_s
            sw = lax.dot_general(qb, kwb[:, ksl], NT_DIMS, preferred_element_type=F32) * ATTN_SCALE
            x = jnp.where(win_ok, sw, NEG_INF)
            m_w = jnp.max(x, axis=-1, keepdims=True)
            pw = jnp.where(win_ok, jnp.exp(x - m_w), 0.0)
            l_w = jnp.sum(pw, axis=-1, keepdims=True)
            acc_w = jnp.dot(pw.astype(BF16), vwb[:, ksl], preferred_element_type=F32)
            _, l_w, acc_w = _tail_keys(qf, wnew[:, ksl], wnew[:, vsl], tok, m_w, l_w, acc_w)
            o_w = acc_w / l_w
            gc = jnp.zeros((rows, 1), F32)
            gs = jnp.zeros((rows, 1), F32)
            gw = jnp.zeros((rows, 1), F32)
            for g in range(NSA_GROUP):
                head = h * NSA_GROUP + g
                gc = jnp.where(grp == g, gates[:, head:head + 1], gc)
                gs = jnp.where(grp == g, gates[:, NSA_HEADS + head:NSA_HEADS + head + 1], gs)
                gw = jnp.where(grp == g, gates[:, 2 * NSA_HEADS + head:2 * NSA_HEADS + head + 1], gw)
            stacks.append(gc * oc_scr[h] + gs * o_s + gw * o_w)
        for sidx in range(NSA_HEADS // 2):
            h = sidx // 2
            halves = []
            for g in (2 * (sidx % 2), 2 * (sidx % 2) + 1):
                part = stacks[h]
                if g > 0:
                    part = pltpu.roll(part, rows - g * nt, axis=0)
                if g % 2 != h % 2:
                    part = pltpu.roll(part, HALF, axis=1)
                halves.append(part)
            slab = jnp.where(lane < HALF, halves[0], halves[1])
            o_ref[0, :, sidx * LANES:(sidx + 1) * LANES] = slab[0:nt]


def nsa_sample(q16, gn16, kvc, sel_pages, page_table_flat, snew, wnew, win_state, *, nt, past):
    bsz, rows, _ = q16.shape
    n_pages = past // PAGE_SIZE
    assert past % SEL_BLOCK == 0 and nt <= SEL_BLOCK and rows == NSA_GROUP * nt
    assert n_pages % PAGES_PER_STEP == 0
    n_steps = n_pages // PAGES_PER_STEP
    step_keys = PAGES_PER_STEP * PAGE_SIZE
    hw = NSA_KV_HEADS * HEAD_DIM
    per_b = lambda shape, cb=0: pl.BlockSpec((1,) + shape, lambda b, p, pt: (b, 0, cb))
    hs = NSA_KV_HEADS
    return pl.pallas_call(
        functools.partial(_nsa_sample_kernel, nt=nt, past=past, n_steps=n_steps),
        grid_spec=pltpu.PrefetchScalarGridSpec(
            num_scalar_prefetch=1,
            grid=(bsz, n_steps),
            in_specs=[
                per_b((rows, Q_COLS)), per_b((rows, GN_PAD)),
                per_b((kvc.shape[1], hw), 0), per_b((kvc.shape[1], hw), 1),
                *_page_specs(PAGE_BLOCK, n_pages),
                per_b((nt, KV_ROW)), per_b((nt, KV_ROW)),
                per_b((win_state.shape[1], KV_ROW)),
            ],
            out_specs=pl.BlockSpec((1, nt, NSA_WIDTH), lambda b, p, pt: (b, 0, 0)),
            scratch_shapes=[
                pltpu.VMEM((hs, rows, LANES), F32), pltpu.VMEM((hs, rows, LANES), F32),
                pltpu.VMEM((hs, rows, LANES), F32), pltpu.VMEM((hs, rows, 1), F32),
                pltpu.VMEM((hs, rows, 1), F32), pltpu.VMEM((hs, rows, LANES), F32),
                pltpu.VMEM((step_keys, hw), BF16), pltpu.VMEM((step_keys, hw), BF16),
            ],
        ),
        out_shape=jax.ShapeDtypeStruct((bsz, nt, NSA_WIDTH), F32),
        compiler_params=_cparams("parallel", "arbitrary"),
        name="nsa_sample",
    )(page_table_flat, q16, gn16, kvc, kvc, *([sel_pages] * PAGES_PER_STEP), snew, wnew, win_state)


def _merge_kernel(za_ref, ob_ref, g0_ref, g1_ref, h_ref, wua_ref, wub_ref, wo_ref, g_ref, b_ref, o_ref, obf_ref):
    ya = jnp.dot(za_ref[...], wua_ref[...], preferred_element_type=F32)
    yb = jnp.dot(ob_ref[...], wub_ref[...], preferred_element_type=F32)
    mixed = jax.nn.sigmoid(g0_ref[...]) * ya + jax.nn.sigmoid(g1_ref[...]) * yb
    mix = jnp.dot(mixed.astype(BF16), wo_ref[...], preferred_element_type=F32)
    o = _layer_norm(DEEPNORM_ALPHA * h_ref[...] + mix, g_ref[...], b_ref[...])
    o_ref[...] = o
    obf_ref[...] = o.astype(BF16)


def merge_ln(za, ob, u, h, wua, wub, wo, g, b, *, tm):
    m, d = h.shape
    W = za.shape[1]
    assert m % tm == 0
    tile = lambda w, cb=0: pl.BlockSpec((tm, w), lambda i: (i, cb))
    const = lambda a: pl.BlockSpec(a.shape, lambda i: (0, 0), pipeline_mode=pl.Buffered(1))
    row = pl.BlockSpec((1, d), lambda i: (0, 0))
    return pl.pallas_call(
        _merge_kernel,
        grid=(m // tm,),
        in_specs=[tile(W), tile(W), tile(d, OFF_GM // d), tile(d, OFF_GM // d + 1), tile(d),
                  const(wua), const(wub), const(wo), row, row],
        out_specs=[tile(d), tile(d)],
        out_shape=[jax.ShapeDtypeStruct((m, d), F32), jax.ShapeDtypeStruct((m, d), BF16)],
        compiler_params=_cparams("parallel"),
        name="merge_ln",
    )(za, ob, u, u, h, wua, wub, wo, g.reshape(1, d), b.reshape(1, d))


TOKEN_TM = 640
PROJ_TN = 2176
MERGE_TM = 320


def kernel(x_prompt, x_sample, cache_cmp_kv, cache_sel_kv, page_table, state_win_kv, state_rwkv, state_rwkv_shift,
           ln1_g, ln1_b, ffn1_w_gu, ffn1_w_down, w_in, rwkv_mu, rwkv_w0, rwkv_w2, rwkv_a0, rwkv_a2, rwkv_g2,
           rwkv_k_k, rwkv_k_a, rwkv_r_k, rwkv_ln_w, rwkv_ln_b, w_up_a, cmp_pe, cmp_wa, cmp_wb, w_up_b, w_o,
           ln2_g, ln2_b, ffn2_w_gu, ffn2_w_down, ln3_g, ln3_b):
    bp, tp, d = x_prompt.shape
    bs, ts, _ = x_sample.shape
    assert bp == 1
    mp, ms = bp * tp, bs * ts
    n_pool = cache_cmp_kv.shape[0]
    n_pages = page_table.shape[1]
    past = n_pages * PAGE_SIZE
    kvh = (2, NSA_KV_HEADS, HEAD_DIM)
    bf = lambda a: a.astype(BF16)
    f32 = lambda a: a.astype(F32)

    x = jnp.concatenate([f32(x_prompt).reshape(mp, d), f32(x_sample).reshape(ms, d)], axis=0)
    h1, h1_bf = ffn_ln(x, bf(ffn1_w_gu), bf(ffn1_w_down), f32(ln1_g), f32(ln1_b), tm=TOKEN_TM)
    u = matmul(h1_bf, bf(pad_in_cols(w_in)), tm=TOKEN_TM, tn=PROJ_TN)
    u_s = u[mp:].reshape(bs, ts, IN_PAD)
    kv_p = u[:mp, OFF_KV:OFF_KV + KV_COLS]
    cmp_p, sel_p, win_p = kv_p[:, 0:KV_ROW], kv_p[:, KV_ROW:2 * KV_ROW], kv_p[:, 2 * KV_ROW:3 * KV_ROW]
    cmp_s = u_s[:, :, OFF_KV:OFF_KV + KV_ROW]
    sel_s = u_s[:, :, OFF_KV + KV_ROW:OFF_KV + 2 * KV_ROW]
    win_s = u_s[:, :, OFF_KV + 2 * KV_ROW:OFF_KV + 3 * KV_ROW]

    rc = rwkv_consts(rwkv_mu, rwkv_w0, rwkv_w2, rwkv_a0, rwkv_a2, rwkv_g2, rwkv_k_k, rwkv_k_a)
    za_p, s_p = rwkv_time_mix(u, 0, mp, bp, tp, None, None, rc, rwkv_r_k, rwkv_ln_w, rwkv_ln_b, prompt=True)
    za_s, s_s = rwkv_time_mix(u, mp, ms, bs, ts, state_rwkv_shift, state_rwkv, rc, rwkv_r_k, rwkv_ln_w, rwkv_ln_b,
                              prompt=False)

    cc = cmp_consts(cmp_pe, cmp_wa, cmp_wb)
    as_pages = lambda a: f32(a).reshape(-1, PAGE_SIZE, KV_ROW)
    kvc_p = compress(as_pages(cmp_p), jnp.arange(mp // PAGE_SIZE, dtype=jnp.int32), 1, mp // PAGE_SIZE, cc)[0]
    o_p = nsa_prompt(u, mp, kvc_p, bf(sel_p), bf(win_p))
    pt_flat = page_table.reshape(-1).astype(jnp.int32)
    kvc_s = compress(as_pages(cache_cmp_kv), pt_flat, bs, n_pages, cc)
    q16 = jnp.tile(u_s[:, :, OFF_Q:OFF_Q + Q_COLS], (1, NSA_GROUP, 1))
    gn16 = jnp.tile(u_s[:, :, OFF_GN:OFF_GN + GN_PAD], (1, NSA_GROUP, 1))
    o_s = nsa_sample(q16, gn16, kvc_s, as_pages(cache_sel_kv), pt_flat, sel_s, win_s,
                     f32(state_win_kv).reshape(bs, -1, KV_ROW), nt=ts, past=past)

    za = jnp.concatenate([za_p, za_s], axis=0)
    ob = jnp.concatenate([o_p, bf(o_s.reshape(ms, NSA_WIDTH))], axis=0)
    h2, _ = merge_ln(za, ob, u, h1, bf(w_up_a), bf(w_up_b), bf(w_o), f32(ln2_g), f32(ln2_b), tm=MERGE_TM)
    y, _ = ffn_ln(h2, bf(ffn2_w_gu), bf(ffn2_w_down), f32(ln3_g), f32(ln3_b), tm=TOKEN_TM)

    wb = min(WINDOW, tp)
    wkeep = min(WINDOW, state_win_kv.shape[1] + ts)
    win_all = jnp.concatenate([state_win_kv, win_s.reshape(bs, ts, *kvh).astype(state_win_kv.dtype)], axis=1)
    cd, sd, wd = cache_cmp_kv.dtype, cache_sel_kv.dtype, state_win_kv.dtype
    rd, hd = state_rwkv.dtype, state_rwkv_shift.dtype
    return (
        y[:mp].reshape(bp, tp, d).astype(x_prompt.dtype),
        y[mp:].reshape(bs, ts, d).astype(x_sample.dtype),
        cmp_p.reshape(bp, tp, *kvh).astype(cd),
        sel_p.reshape(bp, tp, *kvh).astype(sd),
        win_p[tp - wb:].reshape(bp, wb, *kvh).astype(wd),
        s_p.astype(rd),
        u[mp - 1:mp, OFF_UA:OFF_UA + RWKV_COLS].astype(hd),
        cmp_s.reshape(bs, ts, *kvh).astype(cd),
        sel_s.reshape(bs, ts, *kvh).astype(sd),
        win_all[:, win_all.shape[1] - wkeep:].astype(wd),
        s_s.astype(rd),
        u_s[:, ts - 1, OFF_UA:OFF_UA + RWKV_COLS].astype(hd),
    )
```
